```python
import math
import jax, jax.numpy as jnp
from jax import lax
import numpy as np


D_MODEL = 1024
BATCH = 4
SEQ = 4096
DEPTH = 2

EPS = 1e-6
CONV_WIDTH = 4
N_BRANCHES = 3
GMLP_WIDTH = D_MODEL
GMLP_GROUPS = 8
GMLP_GROUP_DIM = GMLP_WIDTH // GMLP_GROUPS
GMLP_CHUNK = 128
LRU_WIDTH = D_MODEL
LRU_HEADS = 8
LRU_HEAD_DIM = LRU_WIDTH // LRU_HEADS
LRU_C = 8.0
SSD_WIDTH = D_MODEL
SSD_HEAD_DIM = 64
SSD_HEADS = SSD_WIDTH // SSD_HEAD_DIM
SSD_GROUPS = 4
SSD_HEADS_PER_GROUP = SSD_HEADS // SSD_GROUPS
SSD_STATE = 128
SSD_CHUNK = 128
SSD_CONV_DIM = SSD_WIDTH + 2 * SSD_GROUPS * SSD_STATE
MLP_HIDDEN = 4 * D_MODEL
GMLP_IN = 2 * GMLP_WIDTH
LRU_IN = 2 * LRU_WIDTH
SSD_IN = SSD_WIDTH + SSD_CONV_DIM + SSD_HEADS
GATE_IN = N_BRANCHES * D_MODEL
D_IN = GMLP_IN + LRU_IN + SSD_IN + GATE_IN
SPLIT_POINTS = (GMLP_IN,
                GMLP_IN + LRU_IN,
                GMLP_IN + LRU_IN + SSD_WIDTH,
                GMLP_IN + LRU_IN + SSD_WIDTH + SSD_CONV_DIM,
                GMLP_IN + LRU_IN + SSD_IN)

kernel_name = 'hybrid_gmlp_rglru_ssd_gated_merge'


def rms_norm(x, g):
    x32 = x.astype(jnp.float32)
    y = x32 * lax.rsqrt(jnp.mean(x32 * x32, axis=-1, keepdims=True) + EPS)
    return (y * g.astype(jnp.float32)).astype(x.dtype)


def layer_norm(x, g, b):
    x32 = x.astype(jnp.float32)
    mu = jnp.mean(x32, axis=-1, keepdims=True)
    xc = x32 - mu
    y = xc * lax.rsqrt(jnp.mean(xc * xc, axis=-1, keepdims=True) + EPS)
    return (y * g.astype(jnp.float32) + b.astype(jnp.float32)).astype(x.dtype)


def causal_dwconv(x, w, b):
    k, c = w.shape
    y = lax.conv_general_dilated(x, w[:, None, :], window_strides=(1,), padding=[(k - 1, 0)],
                                 dimension_numbers=('NWC', 'WIO', 'NWC'), feature_group_count=c)
    return y + b


def gmlp_mixer(za, ln_g, ln_b, w_s, b_s):
    bsz, seq, _ = za.shape
    u, v = jnp.split(jax.nn.gelu(za), 2, axis=-1)
    v = layer_norm(v, ln_g, ln_b)
    nc = seq // GMLP_CHUNK
    vc = v.reshape(bsz, nc, GMLP_CHUNK, GMLP_GROUPS, GMLP_GROUP_DIM)
    causal = jnp.tril(jnp.ones((GMLP_CHUNK, GMLP_CHUNK), dtype=bool))
    w = jnp.where(causal, w_s, 0)
    mixed = jnp.einsum('gts,bcsgd->bctgd', w, vc) + b_s.T[:, :, None]
    return u * mixed.reshape(bsz, seq, GMLP_WIDTH)


def rg_lru_mixer(zb, conv_w, conv_b, w_r, b_r, w_i, b_i, lam):
    bsz, seq, _ = zb.shape
    xb, gate = jnp.split(zb, 2, axis=-1)
    xb = causal_dwconv(xb, conv_w, conv_b)
    xh = xb.reshape(bsz, seq, LRU_HEADS, LRU_HEAD_DIM)
    r = jax.nn.sigmoid(jnp.einsum('bshi,hij->bshj', xh, w_r).reshape(bsz, seq, LRU_WIDTH) + b_r)
    i = jax.nn.sigmoid(jnp.einsum('bshi,hij->bshj', xh, w_i).reshape(bsz, seq, LRU_WIDTH) + b_i)
    log_a = -LRU_C * r.astype(jnp.float32) * jax.nn.softplus(-lam.astype(jnp.float32))
    a = jnp.exp(log_a)
    inp = jnp.sqrt(-jnp.expm1(2.0 * log_a)) * (i * xb).astype(jnp.float32)

    def combine(c1, c2):
        a1, b1 = c1
        a2, b2 = c2
        return a1 * a2, a2 * b1 + b2

    _, h = lax.associative_scan(combine, (a, inp), axis=1)
    return jax.nn.gelu(gate) * h.astype(gate.dtype)


def segsum(x):
    t = x.shape[-1]
    cs = jnp.cumsum(x, axis=-1)
    seg = cs[..., :, None] - cs[..., None, :]
    return jnp.where(jnp.tril(jnp.ones((t, t), dtype=bool)), seg, -jnp.inf)


def ssd_mixer(z, xbc, dt_raw, conv_w, conv_b, dt_bias, a_log, d_skip, norm_g):
    bsz, seq, _ = z.shape
    nc = seq // SSD_CHUNK
    g, r, q = SSD_GROUPS, SSD_HEADS_PER_GROUP, SSD_CHUNK
    xbc = jax.nn.silu(causal_dwconv(xbc, conv_w, conv_b))
    xs, bm, cm = jnp.split(xbc, [SSD_WIDTH, SSD_WIDTH + g * SSD_STATE], axis=-1)
    dt = jax.nn.softplus((dt_raw + dt_bias).astype(jnp.float32))
    a = -jnp.exp(a_log.astype(jnp.float32))
    x32 = xs.astype(jnp.float32).reshape(bsz, nc, q, g, r, SSD_HEAD_DIM)
    xdt = x32 * dt.reshape(bsz, nc, q, g, r)[..., None]
    bc = bm.astype(jnp.float32).reshape(bsz, nc, q, g, SSD_STATE)
    cc = cm.astype(jnp.float32).reshape(bsz, nc, q, g, SSD_STATE)
    adt = (dt * a).reshape(bsz, nc, q, g, r).transpose(0, 3, 4, 1, 2)
    a_cs = jnp.cumsum(adt, axis=-1)
    decay = jnp.exp(segsum(adt))
    cb = jnp.einsum('bclgn,bcsgn->bgcls', cc, bc)
    y_diag = jnp.einsum('bgcls,bgrcls,bcsgrp->bclgrp', cb, decay, xdt)
    decay_states = jnp.exp(a_cs[..., -1:] - a_cs)
    states = jnp.einsum('bcsgn,bgrcs,bcsgrp->cbgrpn', bc, decay_states, xdt)
    chunk_decay = jnp.exp(a_cs[..., -1]).transpose(3, 0, 1, 2)

    def step(h, inp):
        dec, st = inp
        return h * dec[..., None, None] + st, h

    _, prev = lax.scan(step, jnp.zeros_like(states[0]), (chunk_decay, states))
    y_off = jnp.einsum('bclgn,cbgrpn,bgrcl->bclgrp', cc, prev, jnp.exp(a_cs))
    y = y_diag + y_off + x32 * d_skip.astype(jnp.float32).reshape(g, r)[:, :, None]
    y = y.reshape(bsz, seq, SSD_WIDTH) * jax.nn.silu(z.astype(jnp.float32))
    yg = y.reshape(bsz, seq, g, SSD_WIDTH // g)
    yg = yg * lax.rsqrt(jnp.mean(yg * yg, axis=-1, keepdims=True) + EPS)
    y = yg.reshape(bsz, seq, SSD_WIDTH) * norm_g.astype(jnp.float32)
    return y.astype(z.dtype)


def _normal(k, shape, scale):
    return scale * jax.random.normal(k, shape, jnp.float32)


def setup_inputs(seed: int = 0) -> dict:
    key = jax.random.key(seed)
    ks = jax.random.split(key, 32)
    L = DEPTH
    a0 = jax.random.uniform(ks[15], (L, LRU_WIDTH), jnp.float32, minval=0.9, maxval=0.999)
    dt0 = jnp.exp(jax.random.uniform(ks[18], (L, SSD_HEADS), jnp.float32,
                                     minval=math.log(1e-3), maxval=math.log(1e-1)))
    return {
        'x': _normal(ks[0], (BATCH, SEQ, D_MODEL), 1.0),
        'norm_mix_g': 1.0 + _normal(ks[1], (L, D_MODEL), 0.1),
        'w_in': _normal(ks[2], (L, D_MODEL, D_IN), D_MODEL ** -0.5),
        'b_gate': _normal(ks[3], (L, N_BRANCHES, D_MODEL), 0.1),
        'gmlp_ln_g': 1.0 + _normal(ks[4], (L, GMLP_WIDTH), 0.1),
        'gmlp_ln_b': _normal(ks[5], (L, GMLP_WIDTH), 0.1),
        'gmlp_w_s': _normal(ks[6], (L, GMLP_GROUPS, GMLP_CHUNK, GMLP_CHUNK), GMLP_CHUNK ** -0.5),
        'gmlp_b_s': 1.0 + _normal(ks[7], (L, GMLP_GROUPS, GMLP_CHUNK), 0.1),
        'lru_conv_w': _normal(ks[8], (L, CONV_WIDTH, LRU_WIDTH), CONV_WIDTH ** -0.5),
        'lru_conv_b': _normal(ks[9], (L, LRU_WIDTH), 0.1),
        'lru_w_r': _normal(ks[10], (L, LRU_HEADS, LRU_HEAD_DIM, LRU_HEAD_DIM), LRU_HEAD_DIM ** -0.5),
        'lru_b_r': _normal(ks[11], (L, LRU_WIDTH), 0.1),
        'lru_w_i': _normal(ks[12], (L, LRU_HEADS, LRU_HEAD_DIM, LRU_HEAD_DIM), LRU_HEAD_DIM ** -0.5),
        'lru_b_i': _normal(ks[13], (L, LRU_WIDTH), 0.1),
        'lru_lambda': jnp.log(a0) - jnp.log1p(-a0),
        'ssd_conv_w': _normal(ks[16], (L, CONV_WIDTH, SSD_CONV_DIM), CONV_WIDTH ** -0.5),
        'ssd_conv_b': _normal(ks[17], (L, SSD_CONV_DIM), 0.1),
        'ssd_dt_bias': dt0 + jnp.log(-jnp.expm1(-dt0)),
        'ssd_a_log': jnp.log(jax.random.uniform(ks[19], (L, SSD_HEADS), jnp.float32, minval=1.0, maxval=16.0)),
        'ssd_d': 1.0 + _normal(ks[20], (L, SSD_HEADS), 0.1),
        'ssd_norm_g': 1.0 + _normal(ks[21], (L, SSD_WIDTH), 0.1),
        'w_branch_a': _normal(ks[22], (L, GMLP_WIDTH, D_MODEL), GMLP_WIDTH ** -0.5),
        'w_branch_b': _normal(ks[23], (L, LRU_WIDTH, D_MODEL), LRU_WIDTH ** -0.5),
        'w_branch_c': _normal(ks[24], (L, SSD_WIDTH, D_MODEL), SSD_WIDTH ** -0.5),
        'w_out': _normal(ks[25], (L, D_MODEL, D_MODEL), D_MODEL ** -0.5),
        'norm_mlp_g': 1.0 + _normal(ks[26], (L, D_MODEL), 0.1),
        'w_mlp_up': _normal(ks[27], (L, D_MODEL, MLP_HIDDEN), D_MODEL ** -0.5),
        'w_mlp_down': _normal(ks[28], (L, MLP_HIDDEN, D_MODEL), MLP_HIDDEN ** -0.5),
        'final_norm_g': 1.0 + _normal(ks[29], (D_MODEL,), 0.1),
    }


def reference(x, norm_mix_g, w_in, b_gate, gmlp_ln_g, gmlp_ln_b, gmlp_w_s, gmlp_b_s,
              lru_conv_w, lru_conv_b, lru_w_r, lru_b_r, lru_w_i, lru_b_i, lru_lambda,
              ssd_conv_w, ssd_conv_b, ssd_dt_bias, ssd_a_log, ssd_d, ssd_norm_g,
              w_branch_a, w_branch_b, w_branch_c, w_out, norm_mlp_g, w_mlp_up, w_mlp_down,
              final_norm_g):
    bsz, seq, _ = x.shape
    h = x
    for l in range(DEPTH):
        hn = rms_norm(h, norm_mix_g[l])
        proj = jnp.einsum('bsd,de->bse', hn, w_in[l])
        za, zb, zc, xbc, dt_raw, g_raw = jnp.split(proj, SPLIT_POINTS, axis=-1)
        ya = gmlp_mixer(za, gmlp_ln_g[l], gmlp_ln_b[l], gmlp_w_s[l], gmlp_b_s[l])
        yb = rg_lru_mixer(zb, lru_conv_w[l], lru_conv_b[l], lru_w_r[l], lru_b_r[l],
                          lru_w_i[l], lru_b_i[l], lru_lambda[l])
        yc = ssd_mixer(zc, xbc, dt_raw, ssd_conv_w[l], ssd_conv_b[l], ssd_dt_bias[l],
                       ssd_a_log[l], ssd_d[l], ssd_norm_g[l])
        gates = jax.nn.sigmoid(g_raw.reshape(bsz, seq, N_BRANCHES, D_MODEL) + b_gate[l])
        merged = (gates[:, :, 0] * jnp.einsum('bse,ed->bsd', ya, w_branch_a[l])
                  + gates[:, :, 1] * jnp.einsum('bse,ed->bsd', yb, w_branch_b[l])
                  + gates[:, :, 2] * jnp.einsum('bse,ed->bsd', yc, w_branch_c[l]))
        h = h + jnp.einsum('bsd,de->bse', merged, w_out[l])
        hn = rms_norm(h, norm_mlp_g[l])
        up = jax.nn.relu(jnp.einsum('bsd,df->bsf', hn, w_mlp_up[l]))
        h = h + jnp.einsum('bsf,fd->bsd', up * up, w_mlp_down[l])
    return rms_norm(h, final_norm_g)
```

```python
import functools

import jax
import jax.numpy as jnp
from jax import lax
from jax.experimental import pallas as pl
from jax.experimental.pallas import tpu as pltpu

F32 = jnp.float32
BF16 = jnp.bfloat16

LANES = 128
SUBLANES = 8
VMEM_LIMIT_BYTES = 60 * 1024 * 1024

EPS = 1e-6
D_MODEL = 1024
CONV_WIDTH = 4
CHUNK = 128
GMLP_GROUPS = 8
LRU_HEADS = 8
LRU_C = 8.0
SSD_HEADS = 16
SSD_HEAD_DIM = 64
SSD_GROUPS = 4
SSD_HPG = SSD_HEADS // SSD_GROUPS
SSD_STATE = 128
SSD_GW = SSD_HPG * SSD_HEAD_DIM
MLP_HIDDEN = 4 * D_MODEL
MLP_TILE = 512
MLP_HCHUNK = 1024


def _dot(a, b):
    return jnp.dot(a, b, preferred_element_type=F32)


def _rows(i, n):
    return pl.ds(pl.multiple_of(i * n, n), n)


def _shift_rows(x, d, fill):
    row = lax.broadcasted_iota(jnp.int32, x.shape, 0)
    return jnp.where(row >= d, pltpu.roll(x, d, 0), fill)


def _causal_conv_block(buf_ref, i, cols, w_ref, b_ref):
    blk = buf_ref[pl.ds(pl.multiple_of(i * SUBLANES, SUBLANES), 2 * SUBLANES), cols]
    acc = blk[SUBLANES:] * w_ref[CONV_WIDTH - 1:CONV_WIDTH, cols] + b_ref[:, cols]
    for k in range(1, CONV_WIDTH):
        tap = pltpu.roll(blk, k, 0)[SUBLANES:]
        acc = acc + tap * w_ref[CONV_WIDTH - 1 - k:CONV_WIDTH - k, cols]
    return acc


def _mixer_kernel(x_ref, ng_ref, wa_ref, wb_ref, wz_ref, wxbc_ref, wdt_ref, wg_ref,
                  bgate_ref, lng_ref, lnb_ref, ws_ref, bsfull_ref,
                  lcw_ref, lcb_ref, wri_ref, br_ref, bi_ref, lam_ref,
                  scw_ref, scb_ref, dtb_ref, alog_ref, dskip_ref, sng_ref,
                  wba_ref, wbb_ref, wbc_ref, wo_ref,
                  o_ref,
                  hn_s, za_s, v_s, ya_s, xb_s, gate_s, xc_s, ri_s, yb_s,
                  z_s, xbc_s, xbcc_s, dt_s, y_s, yc_s, graw_s, p_s, mg_s,
                  lruh_s, state_s):
    n_blk8 = CHUNK // SUBLANES
    n_blk16 = CHUNK // (2 * SUBLANES)

    @pl.when(pl.program_id(1) == 0)
    def _reset_state():
        lruh_s[...] = jnp.zeros_like(lruh_s)
        state_s[...] = jnp.zeros_like(state_s)
        xb_s[0:SUBLANES, :] = jnp.zeros((SUBLANES, xb_s.shape[1]), F32)
        xbc_s[0:SUBLANES, :] = jnp.zeros((SUBLANES, xbc_s.shape[1]), F32)

    def norm_body(i, c):
        rows = _rows(i, 2 * SUBLANES)
        x = x_ref[0, rows, :]
        ms = jnp.mean(x * x, axis=-1, keepdims=True)
        hn_s[rows, :] = (x * lax.rsqrt(ms + EPS) * ng_ref[...]).astype(BF16)
        return c
    lax.fori_loop(0, n_blk16, norm_body, 0)
    hn = hn_s[...]

    za_s[...] = _dot(hn, wa_ref[...])

    def gmlp_body(i, c):
        rows = _rows(i, SUBLANES)
        u = jax.nn.gelu(za_s[rows, 0:D_MODEL])
        v = jax.nn.gelu(za_s[rows, D_MODEL:2 * D_MODEL])
        mu = jnp.mean(v, axis=-1, keepdims=True)
        vc = v - mu
        var = jnp.mean(vc * vc, axis=-1, keepdims=True)
        za_s[rows, 0:D_MODEL] = u
        v_s[rows, :] = vc * lax.rsqrt(var + EPS) * lng_ref[...] + lnb_ref[...]
        return c
    lax.fori_loop(0, n_blk8, gmlp_body, 0)

    tril = (lax.broadcasted_iota(jnp.int32, (CHUNK, CHUNK), 0)
            >= lax.broadcasted_iota(jnp.int32, (CHUNK, CHUNK), 1))
    for g in range(GMLP_GROUPS):
        cols = slice(g * LANES, (g + 1) * LANES)
        w = jnp.where(tril, ws_ref[g], 0.0).astype(BF16)
        mixed = _dot(w, v_s[:, cols].astype(BF16)) + bsfull_ref[:, cols]
        ya_s[:, cols] = (za_s[:, cols] * mixed).astype(BF16)

    xb_s[SUBLANES:, :] = _dot(hn, wb_ref[:, 0:D_MODEL])
    gate_s[...] = _dot(hn, wb_ref[:, D_MODEL:2 * D_MODEL])

    def lru_conv_body(i, c):
        xc_s[_rows(i, SUBLANES), :] = _causal_conv_block(
            xb_s, i, slice(0, D_MODEL), lcw_ref, lcb_ref)
        return c
    lax.fori_loop(0, n_blk8, lru_conv_body, 0)
    xb_s[0:SUBLANES, :] = xb_s[CHUNK:CHUNK + SUBLANES, :]

    for h in range(LRU_HEADS):
        cols = slice(h * LANES, (h + 1) * LANES)
        ri_s[:, 2 * h * LANES:2 * (h + 1) * LANES] = _dot(xc_s[:, cols].astype(BF16), wri_ref[h])

    neg_c_softplus = -LRU_C * jax.nn.softplus(-lam_ref[...])

    def lru_scan_body(i, h_prev):
        rows = _rows(i, SUBLANES)
        ri = ri_s[rows, :]
        r_pre = jnp.concatenate(
            [ri[:, 2 * h * LANES:(2 * h + 1) * LANES] for h in range(LRU_HEADS)], axis=1)
        i_pre = jnp.concatenate(
            [ri[:, (2 * h + 1) * LANES:(2 * h + 2) * LANES] for h in range(LRU_HEADS)], axis=1)
        r = jax.nn.sigmoid(r_pre + br_ref[...])
        ig = jax.nn.sigmoid(i_pre + bi_ref[...])
        xc = xc_s[rows, :]
        log_a = r * neg_c_softplus
        a_cum = jnp.exp(log_a)
        b_cum = jnp.sqrt(-jnp.tanh(log_a) * (a_cum * a_cum + 1.0)) * (ig * xc)
        for d in (1, 2, 4):
            b_cum = a_cum * _shift_rows(b_cum, d, 0.0) + b_cum
            a_cum = a_cum * _shift_rows(a_cum, d, 1.0)
        hseq = a_cum * h_prev + b_cum
        yb_s[rows, :] = jax.nn.gelu(gate_s[rows, :]) * hseq
        return jnp.broadcast_to(hseq[SUBLANES - 1:SUBLANES, :], hseq.shape)
    lruh_s[...] = lax.fori_loop(0, n_blk8, lru_scan_body, lruh_s[...])

    z_s[...] = _dot(hn, wz_ref[...])
    xbc_s[SUBLANES:, :] = _dot(hn, wxbc_ref[...])
    dt_s[...] = _dot(hn, wdt_ref[...])

    def ssd_conv_body(i, c):
        rows = _rows(i, SUBLANES)
        for q in range(4):
            cols = slice(q * 512, (q + 1) * 512)
            xbcc_s[rows, cols] = jax.nn.silu(_causal_conv_block(xbc_s, i, cols, scw_ref, scb_ref))
        return c
    lax.fori_loop(0, n_blk8, ssd_conv_body, 0)
    xbc_s[0:SUBLANES, :] = xbc_s[CHUNK:CHUNK + SUBLANES, :]

    dt = jax.nn.softplus(dt_s[...] + dtb_ref[...])
    adt = dt * (-jnp.exp(alog_ref[...]))
    cs = adt
    for d in (1, 2, 4, 8, 16, 32, 64):
        cs = cs + _shift_rows(cs, d, 0.0)
    cs_t = cs.T
    dt_t = dt.T
    e_cs = jnp.exp(cs)
    w_t = dt_t * jnp.exp(cs_t[:, CHUNK - 1:CHUNK] - cs_t)
    lane_gw = lax.broadcasted_iota(jnp.int32, (CHUNK, SSD_GW), 1)
    lane_row = lax.broadcasted_iota(jnp.int32, (1, SSD_GW), 1)

    for g in range(SSD_GROUPS):
        xcols = slice(g * SSD_GW, (g + 1) * SSD_GW)
        bcols = slice(D_MODEL + g * SSD_STATE, D_MODEL + (g + 1) * SSD_STATE)
        ccols = slice(D_MODEL + SSD_GROUPS * SSD_STATE + g * SSD_STATE,
                      D_MODEL + SSD_GROUPS * SSD_STATE + (g + 1) * SSD_STATE)
        xg = xbcc_s[:, xcols]
        bg = xbcc_s[:, bcols]
        cg = xbcc_s[:, ccols]
        state = state_s[g]
        cb = lax.dot_general(cg.astype(BF16), bg.astype(BF16), (((1,), (1,)), ((), ())),
                             preferred_element_type=F32)
        bg_t = bg.T
        m_parts, ce_parts, bw_parts, x_parts, st_parts = [], [], [], [], []
        dec_row = jnp.zeros((1, SSD_GW), F32)
        for r in range(SSD_HPG):
            h = g * SSD_HPG + r
            seg = jnp.where(tril, cs[:, h:h + 1] - cs_t[h:h + 1, :], -jnp.inf)
            m_parts.append((cb * jnp.exp(seg) * dt_t[h:h + 1, :]).astype(BF16))
            ce_parts.append((cg * e_cs[:, h:h + 1]).astype(BF16))
            bw_parts.append((bg_t * w_t[h:h + 1, :]).astype(BF16))
            in_head = (lane_gw >= r * SSD_HEAD_DIM) & (lane_gw < (r + 1) * SSD_HEAD_DIM)
            x_parts.append(jnp.where(in_head, xg, 0.0).astype(BF16))
            st_parts.append(jnp.where(in_head, state, 0.0).astype(BF16))
            in_head_row = (lane_row >= r * SSD_HEAD_DIM) & (lane_row < (r + 1) * SSD_HEAD_DIM)
            dec_row = jnp.where(in_head_row, e_cs[CHUNK - 1:CHUNK, h:h + 1], dec_row)
        x_blk = jnp.concatenate(x_parts, axis=0)
        lhs = jnp.concatenate(m_parts + ce_parts, axis=1)
        rhs = jnp.concatenate([x_blk] + st_parts, axis=0)
        y_s[:, xcols] = _dot(lhs, rhs)
        state_s[g] = state * dec_row + _dot(jnp.concatenate(bw_parts, axis=1), x_blk)

    def ssd_out_body(i, c):
        rows = _rows(i, SUBLANES)
        y = y_s[rows, :] + xbcc_s[rows, 0:D_MODEL] * dskip_ref[...]
        y = y * jax.nn.silu(z_s[rows, :])
        parts = []
        for g in range(SSD_GROUPS):
            yg = y[:, g * SSD_GW:(g + 1) * SSD_GW]
            parts.append(yg * lax.rsqrt(jnp.mean(yg * yg, axis=-1, keepdims=True) + EPS))
        yc_s[rows, :] = jnp.concatenate(parts, axis=1) * sng_ref[...]
        return c
    lax.fori_loop(0, n_blk8, ssd_out_body, 0)

    graw_s[...] = _dot(hn, wg_ref[...])
    p_s[:, 0:D_MODEL] = _dot(ya_s[...], wba_ref[...])
    p_s[:, D_MODEL:2 * D_MODEL] = _dot(yb_s[...].astype(BF16), wbb_ref[...])
    p_s[:, 2 * D_MODEL:3 * D_MODEL] = _dot(yc_s[...].astype(BF16), wbc_ref[...])

    def merge_body(i, c):
        rows = _rows(i, SUBLANES)
        acc = jnp.zeros((SUBLANES, D_MODEL), F32)
        for k in range(3):
            cols = slice(k * D_MODEL, (k + 1) * D_MODEL)
            acc = acc + jax.nn.sigmoid(graw_s[rows, cols] + bgate_ref[:, cols]) * p_s[rows, cols]
        mg_s[rows, :] = acc
        return c
    lax.fori_loop(0, n_blk8, merge_body, 0)
    o_ref[0] = x_ref[0] + _dot(mg_s[...].astype(BF16), wo_ref[...])


def _mlp_kernel(x_ref, ng_ref, wup_ref, wdn_ref, fg_ref, o_ref, hn_s, *, final_norm):
    n_blk16 = MLP_TILE // (2 * SUBLANES)

    def norm_body(i, c):
        rows = _rows(i, 2 * SUBLANES)
        x = x_ref[rows, :]
        ms = jnp.mean(x * x, axis=-1, keepdims=True)
        hn_s[rows, :] = (x * lax.rsqrt(ms + EPS) * ng_ref[...]).astype(BF16)
        return c
    lax.fori_loop(0, n_blk16, norm_body, 0)
    hn = hn_s[...]
    acc = x_ref[...]
    for c in range(MLP_HIDDEN // MLP_HCHUNK):
        cols = slice(c * MLP_HCHUNK, (c + 1) * MLP_HCHUNK)
        up = jnp.maximum(_dot(hn, wup_ref[:, cols]), 0.0)
        acc = acc + _dot((up * up).astype(BF16), wdn_ref[cols, :])
    if final_norm:
        ms = jnp.mean(acc * acc, axis=-1, keepdims=True)
        acc = acc * lax.rsqrt(ms + EPS) * fg_ref[...]
    o_ref[...] = acc


def _const_spec(shape, single_buffer=False):
    index_map = lambda *_: (0,) * len(shape)
    if single_buffer:
        return pl.BlockSpec(shape, index_map, pipeline_mode=pl.Buffered(1))
    return pl.BlockSpec(shape, index_map)


def _mixer_call(h, params):
    bsz, seq, _ = h.shape
    x_spec = pl.BlockSpec((1, CHUNK, D_MODEL), lambda b, t: (b, t, 0))
    in_specs = [x_spec] + [_const_spec(p.shape, single_buffer=True) for p in params]
    scratch = [
        pltpu.VMEM((CHUNK, D_MODEL), BF16),
        pltpu.VMEM((CHUNK, 2 * D_MODEL), F32),
        pltpu.VMEM((CHUNK, D_MODEL), F32),
        pltpu.VMEM((CHUNK, D_MODEL), BF16),
        pltpu.VMEM((CHUNK + SUBLANES, D_MODEL), F32),
        pltpu.VMEM((CHUNK, D_MODEL), F32),
        pltpu.VMEM((CHUNK, D_MODEL), F32),
        pltpu.VMEM((CHUNK, 2 * D_MODEL), F32),
        pltpu.VMEM((CHUNK, D_MODEL), F32),
        pltpu.VMEM((CHUNK, D_MODEL), F32),
        pltpu.VMEM((CHUNK + SUBLANES, 2 * D_MODEL), F32),
        pltpu.VMEM((CHUNK, 2 * D_MODEL), F32),
        pltpu.VMEM((CHUNK, LANES), F32),
        pltpu.VMEM((CHUNK, D_MODEL), F32),
        pltpu.VMEM((CHUNK, D_MODEL), F32),
        pltpu.VMEM((CHUNK, 3 * D_MODEL), F32),
        pltpu.VMEM((CHUNK, 3 * D_MODEL), F32),
        pltpu.VMEM((CHUNK, D_MODEL), F32),
        pltpu.VMEM((SUBLANES, D_MODEL), F32),
        pltpu.VMEM((SSD_GROUPS, SSD_STATE, SSD_GW), F32),
    ]
    return pl.pallas_call(
        _mixer_kernel,
        out_shape=jax.ShapeDtypeStruct(h.shape, F32),
        grid=(bsz, seq // CHUNK),
        in_specs=in_specs,
        out_specs=x_spec,
        scratch_shapes=scratch,
        compiler_params=pltpu.CompilerParams(
            dimension_semantics=("arbitrary", "arbitrary"),
            vmem_limit_bytes=VMEM_LIMIT_BYTES),
        name="mixer",
    )(h, *params)


def _mlp_call(h2d, ng, wup, wdn, fg, final_norm):
    n_tok = h2d.shape[0]
    x_spec = pl.BlockSpec((MLP_TILE, D_MODEL), lambda i: (i, 0))
    return pl.pallas_call(
        functools.partial(_mlp_kernel, final_norm=final_norm),
        out_shape=jax.ShapeDtypeStruct(h2d.shape, F32),
        grid=(n_tok // MLP_TILE,),
        in_specs=[x_spec, _const_spec(ng.shape), _const_spec(wup.shape, True),
                  _const_spec(wdn.shape, True), _const_spec(fg.shape)],
        out_specs=x_spec,
        scratch_shapes=[pltpu.VMEM((MLP_TILE, D_MODEL), BF16)],
        compiler_params=pltpu.CompilerParams(
            dimension_semantics=("arbitrary",),
            vmem_limit_bytes=VMEM_LIMIT_BYTES),
        name="mlp",
    )(h2d, ng, wup, wdn, fg)


def _row(v):
    return v.reshape(1, -1).astype(F32)


def _pad_lanes(v):
    return jnp.pad(v.astype(F32), ((0, 0), (0, LANES - v.shape[-1])))


def kernel(x, norm_mix_g, w_in, b_gate, gmlp_ln_g, gmlp_ln_b, gmlp_w_s, gmlp_b_s, lru_conv_w, lru_conv_b, lru_w_r, lru_b_r, lru_w_i, lru_b_i, lru_lambda, ssd_conv_w, ssd_conv_b, ssd_dt_bias, ssd_a_log, ssd_d, ssd_norm_g, w_branch_a, w_branch_b, w_branch_c, w_out, norm_mlp_g, w_mlp_up, w_mlp_down, final_norm_g):
    bsz, seq, d = x.shape
    assert d == D_MODEL and seq % CHUNK == 0 and (bsz * seq) % MLP_TILE == 0
    depth = w_in.shape[0]
    c_a, c_b, c_z, c_xbc = 2 * D_MODEL, 2 * D_MODEL, D_MODEL, 2 * D_MODEL
    o_b = c_a
    o_z = o_b + c_b
    o_xbc = o_z + c_z
    o_dt = o_xbc + c_xbc
    o_g = o_dt + SSD_HEADS
    h = x
    for l in range(depth):
        wl = w_in[l]
        params = (
            _row(norm_mix_g[l]),
            wl[:, 0:o_b].astype(BF16),
            wl[:, o_b:o_z].astype(BF16),
            wl[:, o_z:o_xbc].astype(BF16),
            wl[:, o_xbc:o_dt].astype(BF16),
            _pad_lanes(wl[:, o_dt:o_g]).astype(BF16),
            wl[:, o_g:].astype(BF16),
            _row(b_gate[l]),
            _row(gmlp_ln_g[l]),
            _row(gmlp_ln_b[l]),
            gmlp_w_s[l],
            jnp.repeat(gmlp_b_s[l].T, LANES, axis=1),
            lru_conv_w[l],
            _row(lru_conv_b[l]),
            jnp.concatenate([lru_w_r[l], lru_w_i[l]], axis=-1).astype(BF16),
            _row(lru_b_r[l]),
            _row(lru_b_i[l]),
            _row(lru_lambda[l]),
            ssd_conv_w[l],
            _row(ssd_conv_b[l]),
            _pad_lanes(_row(ssd_dt_bias[l])),
            _pad_lanes(_row(ssd_a_log[l])),
            _row(jnp.repeat(ssd_d[l], SSD_HEAD_DIM)),
            _row(ssd_norm_g[l]),
            w_branch_a[l].astype(BF16),
            w_branch_b[l].astype(BF16),
            w_branch_c[l].astype(BF16),
            w_out[l].astype(BF16),
        )
        h = _mixer_call(h, params)
        h2d = _mlp_call(h.reshape(bsz * seq, d), _row(norm_mlp_g[l]),
                        w_mlp_up[l].astype(BF16), w_mlp_down[l].astype(BF16),
                        _row(final_norm_g), final_norm=(l == depth - 1))
        h = h2d.reshape(bsz, seq, d)
    return h
```

```python
import functools

import jax
import jax.numpy as jnp
from jax import lax
from jax.experimental import pallas as pl
from jax.experimental.pallas import tpu as pltpu

F32 = jnp.float32
BF16 = jnp.bfloat16

LANES = 128
SUBLANES = 8
VMEM_LIMIT_BYTES = 60 * 1024 * 1024

EPS = 1e-6
D_MODEL = 1024
CONV_WIDTH = 4
CHUNK = 128
GMLP_GROUPS = 8
LRU_HEADS = 8
LRU_C = 8.0
SSD_HEADS = 16
SSD_HEAD_DIM = 64
SSD_GROUPS = 4
SSD_HPG = SSD_HEADS // SSD_GROUPS
SSD_STATE = 128
SSD_GW = SSD_HPG * SSD_HEAD_DIM
MLP_HIDDEN = 4 * D_MODEL
CONV_COLS = 256
MLP_TILE = 512
MLP_HCHUNK = 1024


def _dot(a, b):
    return jnp.dot(a, b, preferred_element_type=F32)


def _rows(i, n):
    return slice(i * n, (i + 1) * n)


def _unrolled(n, body, carry):
    for i in range(n):
        carry = body(i, carry)
    return carry


def _shift_rows(x, d, fill):
    row = lax.broadcasted_iota(jnp.int32, x.shape, 0)
    return jnp.where(row >= d, pltpu.roll(x, d, 0), fill)


def _causal_conv_cols(buf_ref, cols, w_ref, b_ref):
    blk = buf_ref[:, cols]
    acc = blk[SUBLANES:] * w_ref[CONV_WIDTH - 1:CONV_WIDTH, cols] + b_ref[:, cols]
    for k in range(1, CONV_WIDTH):
        tap = pltpu.roll(blk, k, 0)[SUBLANES:]
        acc = acc + tap * w_ref[CONV_WIDTH - 1 - k:CONV_WIDTH - k, cols]
    return acc


def _mixer_kernel(x_ref, ng_ref, wa_ref, wb_ref, wz_ref, wxbc_ref, wdt_ref, wg_ref,
                  bgate_ref, lng_ref, lnb_ref, ws_ref, bsfull_ref,
                  lcw_ref, lcb_ref, wri_ref, br_ref, bi_ref, lam_ref,
                  scw_ref, scb_ref, dtb_ref, alog_ref, dskip_ref, sng_ref,
                  wba_ref, wbb_ref, wbc_ref, wo_ref,
                  o_ref,
                  hn_s, za_s, v_s, ya_s, xb_s, gate_s, xc_s, ri_s, yb_s,
                  z_s, xbc_s, xbcc_s, dt_s, y_s, yc_s, graw_s, p_s, mg_s,
                  lruh_s, state_s):
    n_blk8 = CHUNK // SUBLANES
    n_blk16 = CHUNK // (2 * SUBLANES)

    @pl.when(pl.program_id(1) == 0)
    def _reset_state():
        lruh_s[...] = jnp.zeros_like(lruh_s)
        state_s[...] = jnp.zeros_like(state_s)
        xb_s[0:SUBLANES, :] = jnp.zeros((SUBLANES, xb_s.shape[1]), F32)
        xbc_s[0:SUBLANES, :] = jnp.zeros((SUBLANES, xbc_s.shape[1]), F32)

    def norm_body(i, c):
        parts = []
        for half in range(2):
            x = x_ref[0, _rows(2 * i + half, SUBLANES), :]
            ms = jnp.mean(x * x, axis=-1, keepdims=True)
            parts.append(x * lax.rsqrt(ms + EPS) * ng_ref[...])
        hn_s[_rows(i, 2 * SUBLANES), :] = jnp.concatenate(parts, axis=0).astype(BF16)
        return c
    _unrolled(n_blk16, norm_body, 0)
    hn = hn_s[...]

    za_s[...] = _dot(hn, wa_ref[...])

    def gmlp_body(i, c):
        rows = _rows(i, SUBLANES)
        u = jax.nn.gelu(za_s[rows, 0:D_MODEL])
        v = jax.nn.gelu(za_s[rows, D_MODEL:2 * D_MODEL])
        mu = jnp.mean(v, axis=-1, keepdims=True)
        vc = v - mu
        var = jnp.mean(vc * vc, axis=-1, keepdims=True)
        za_s[rows, 0:D_MODEL] = u
        v_s[rows, :] = vc * lax.rsqrt(var + EPS) * lng_ref[...] + lnb_ref[...]
        return c
    _unrolled(n_blk8, gmlp_body, 0)

    tril = (lax.broadcasted_iota(jnp.int32, (CHUNK, CHUNK), 0)
            >= lax.broadcasted_iota(jnp.int32, (CHUNK, CHUNK), 1))
    for g in range(GMLP_GROUPS):
        cols = slice(g * LANES, (g + 1) * LANES)
        w = jnp.where(tril, ws_ref[g], 0.0).astype(BF16)
        mixed = _dot(w, v_s[:, cols].astype(BF16)) + bsfull_ref[:, cols]
        ya_s[:, cols] = (za_s[:, cols] * mixed).astype(BF16)

    xb_s[SUBLANES:, :] = _dot(hn, wb_ref[:, 0:D_MODEL])
    gate_s[...] = _dot(hn, wb_ref[:, D_MODEL:2 * D_MODEL])

    for q in range(D_MODEL // CONV_COLS):
        cols = slice(q * CONV_COLS, (q + 1) * CONV_COLS)
        xc_s[:, cols] = _causal_conv_cols(xb_s, cols, lcw_ref, lcb_ref)
    xb_s[0:SUBLANES, :] = xb_s[CHUNK:CHUNK + SUBLANES, :]

    for h in range(LRU_HEADS):
        cols = slice(h * LANES, (h + 1) * LANES)
        ri_s[:, 2 * h * LANES:2 * (h + 1) * LANES] = _dot(xc_s[:, cols].astype(BF16), wri_ref[h])

    neg_c_softplus = -LRU_C * jax.nn.softplus(-lam_ref[...])

    def lru_scan_body(i, h_prev):
        rows = _rows(i, SUBLANES)
        ri = ri_s[rows, :]
        r_pre = jnp.concatenate(
            [ri[:, 2 * h * LANES:(2 * h + 1) * LANES] for h in range(LRU_HEADS)], axis=1)
        i_pre = jnp.concatenate(
            [ri[:, (2 * h + 1) * LANES:(2 * h + 2) * LANES] for h in range(LRU_HEADS)], axis=1)
        r = jax.nn.sigmoid(r_pre + br_ref[...])
        ig = jax.nn.sigmoid(i_pre + bi_ref[...])
        xc = xc_s[rows, :]
        log_a = r * neg_c_softplus
        a_cum = jnp.exp(log_a)
        b_cum = jnp.sqrt(-jnp.tanh(log_a) * (a_cum * a_cum + 1.0)) * (ig * xc)
        for d in (1, 2, 4):
            b_cum = a_cum * _shift_rows(b_cum, d, 0.0) + b_cum
            a_cum = a_cum * _shift_rows(a_cum, d, 1.0)
        hseq = a_cum * h_prev + b_cum
        yb_s[rows, :] = jax.nn.gelu(gate_s[rows, :]) * hseq
        return jnp.broadcast_to(hseq[SUBLANES - 1:SUBLANES, :], hseq.shape)
    lruh_s[...] = _unrolled(n_blk8, lru_scan_body, lruh_s[...])

    z_s[...] = _dot(hn, wz_ref[...])
    xbc_s[SUBLANES:, :] = _dot(hn, wxbc_ref[...])
    dt_s[...] = _dot(hn, wdt_ref[...])

    for q in range(2 * D_MODEL // CONV_COLS):
        cols = slice(q * CONV_COLS, (q + 1) * CONV_COLS)
        xbcc_s[:, cols] = jax.nn.silu(_causal_conv_cols(xbc_s, cols, scw_ref, scb_ref))
    xbc_s[0:SUBLANES, :] = xbc_s[CHUNK:CHUNK + SUBLANES, :]

    dt = jax.nn.softplus(dt_s[...] + dtb_ref[...])
    adt = dt * (-jnp.exp(alog_ref[...]))
    cs = adt
    for d in (1, 2, 4, 8, 16, 32, 64):
        cs = cs + _shift_rows(cs, d, 0.0)
    cs_t = cs.T
    dt_t = dt.T
    e_cs = jnp.exp(cs)
    w_t = dt_t * jnp.exp(cs_t[:, CHUNK - 1:CHUNK] - cs_t)
    lane_gw = lax.broadcasted_iota(jnp.int32, (CHUNK, SSD_GW), 1)
    lane_row = lax.broadcasted_iota(jnp.int32, (1, SSD_GW), 1)

    for g in range(SSD_GROUPS):
        xcols = slice(g * SSD_GW, (g + 1) * SSD_GW)
        bcols = slice(D_MODEL + g * SSD_STATE, D_MODEL + (g + 1) * SSD_STATE)
        ccols = slice(D_MODEL + SSD_GROUPS * SSD_STATE + g * SSD_STATE,
                      D_MODEL + SSD_GROUPS * SSD_STATE + (g + 1) * SSD_STATE)
        xg = xbcc_s[:, xcols]
        bg = xbcc_s[:, bcols]
        cg = xbcc_s[:, ccols]
        state = state_s[g]
        cb = lax.dot_general(cg.astype(BF16), bg.astype(BF16), (((1,), (1,)), ((), ())),
                             preferred_element_type=F32)
        bg_t = bg.T
        m_parts, ce_parts, bw_parts, x_parts, st_parts = [], [], [], [], []
        dec_row = jnp.zeros((1, SSD_GW), F32)
        for r in range(SSD_HPG):
            h = g * SSD_HPG + r
            seg = jnp.where(tril, cs[:, h:h + 1] - cs_t[h:h + 1, :], -jnp.inf)
            m_parts.append((cb * jnp.exp(seg) * dt_t[h:h + 1, :]).astype(BF16))
            ce_parts.append((cg * e_cs[:, h:h + 1]).astype(BF16))
            bw_parts.append((bg_t * w_t[h:h + 1, :]).astype(BF16))
            in_head = (lane_gw >= r * SSD_HEAD_DIM) & (lane_gw < (r + 1) * SSD_HEAD_DIM)
            x_parts.append(jnp.where(in_head, xg, 0.0).astype(BF16))
            st_parts.append(jnp.where(in_head, state, 0.0).astype(BF16))
            in_head_row = (lane_row >= r * SSD_HEAD_DIM) & (lane_row < (r + 1) * SSD_HEAD_DIM)
            dec_row = jnp.where(in_head_row, e_cs[CHUNK - 1:CHUNK, h:h + 1], dec_row)
        x_blk = jnp.concatenate(x_parts, axis=0)
        lhs = jnp.concatenate(m_parts + ce_parts, axis=1)
        rhs = jnp.concatenate([x_blk] + st_parts, axis=0)
        y_s[:, xcols] = _dot(lhs, rhs)
        state_s[g] = state * dec_row + _dot(jnp.concatenate(bw_parts, axis=1), x_blk)

    def ssd_out_body(i, c):
        rows = _rows(i, SUBLANES)
        y = y_s[rows, :] + xbcc_s[rows, 0:D_MODEL] * dskip_ref[...]
        y = y * jax.nn.silu(z_s[rows, :])
        parts = []
        for g in range(SSD_GROUPS):
            yg = y[:, g * SSD_GW:(g + 1) * SSD_GW]
            parts.append(yg * lax.rsqrt(jnp.mean(yg * yg, axis=-1, keepdims=True) + EPS))
        yc_s[rows, :] = jnp.concatenate(parts, axis=1) * sng_ref[...]
        return c
    _unrolled(n_blk8, ssd_out_body, 0)

    graw_s[...] = _dot(hn, wg_ref[...])
    p_s[:, 0:D_MODEL] = _dot(ya_s[...], wba_ref[...])
    p_s[:, D_MODEL:2 * D_MODEL] = _dot(yb_s[...].astype(BF16), wbb_ref[...])
    p_s[:, 2 * D_MODEL:3 * D_MODEL] = _dot(yc_s[...].astype(BF16), wbc_ref[...])

    def merge_body(i, c):
        rows = _rows(i, SUBLANES)
        acc = jnp.zeros((SUBLANES, D_MODEL), F32)
        for k in range(3):
            cols = slice(k * D_MODEL, (k + 1) * D_MODEL)
            acc = acc + jax.nn.sigmoid(graw_s[rows, cols] + bgate_ref[:, cols]) * p_s[rows, cols]
        mg_s[rows, :] = acc
        return c
    _unrolled(n_blk8, merge_body, 0)
    o_ref[0] = x_ref[0] + _dot(mg_s[...].astype(BF16), wo_ref[...])


def _mlp_kernel(x_ref, ng_ref, wup_ref, wdn_ref, fg_ref, o_ref, hn_s, *, final_norm):
    n_blk16 = MLP_TILE // (2 * SUBLANES)

    def norm_body(i, c):
        parts = []
        for half in range(2):
            x = x_ref[_rows(2 * i + half, SUBLANES), :]
            ms = jnp.mean(x * x, axis=-1, keepdims=True)
            parts.append(x * lax.rsqrt(ms + EPS) * ng_ref[...])
        hn_s[_rows(i, 2 * SUBLANES), :] = jnp.concatenate(parts, axis=0).astype(BF16)
        return c
    _unrolled(n_blk16, norm_body, 0)
    hn = hn_s[...]
    acc = x_ref[...]
    for c in range(MLP_HIDDEN // MLP_HCHUNK):
        cols = slice(c * MLP_HCHUNK, (c + 1) * MLP_HCHUNK)
        up = jnp.maximum(_dot(hn, wup_ref[:, cols]), 0.0)
        acc = acc + _dot((up * up).astype(BF16), wdn_ref[cols, :])
    if final_norm:
        ms = jnp.mean(acc * acc, axis=-1, keepdims=True)
        acc = acc * lax.rsqrt(ms + EPS) * fg_ref[...]
    o_ref[...] = acc


def _const_spec(shape, single_buffer=False):
    index_map = lambda *_: (0,) * len(shape)
    if single_buffer:
        return pl.BlockSpec(shape, index_map, pipeline_mode=pl.Buffered(1))
    return pl.BlockSpec(shape, index_map)


def _mixer_call(h, params):
    bsz, seq, _ = h.shape
    x_spec = pl.BlockSpec((1, CHUNK, D_MODEL), lambda b, t: (b, t, 0))
    in_specs = [x_spec] + [_const_spec(p.shape, single_buffer=True) for p in params]
    scratch = [
        pltpu.VMEM((CHUNK, D_MODEL), BF16),
        pltpu.VMEM((CHUNK, 2 * D_MODEL), F32),
        pltpu.VMEM((CHUNK, D_MODEL), F32),
        pltpu.VMEM((CHUNK, D_MODEL), BF16),
        pltpu.VMEM((CHUNK + SUBLANES, D_MODEL), F32),
        pltpu.VMEM((CHUNK, D_MODEL), F32),
        pltpu.VMEM((CHUNK, D_MODEL), F32),
        pltpu.VMEM((CHUNK, 2 * D_MODEL), F32),
        pltpu.VMEM((CHUNK, D_MODEL), F32),
        pltpu.VMEM((CHUNK, D_MODEL), F32),
        pltpu.VMEM((CHUNK + SUBLANES, 2 * D_MODEL), F32),
        pltpu.VMEM((CHUNK, 2 * D_MODEL), F32),
        pltpu.VMEM((CHUNK, LANES), F32),
        pltpu.VMEM((CHUNK, D_MODEL), F32),
        pltpu.VMEM((CHUNK, D_MODEL), F32),
        pltpu.VMEM((CHUNK, 3 * D_MODEL), F32),
        pltpu.VMEM((CHUNK, 3 * D_MODEL), F32),
        pltpu.VMEM((CHUNK, D_MODEL), F32),
        pltpu.VMEM((SUBLANES, D_MODEL), F32),
        pltpu.VMEM((SSD_GROUPS, SSD_STATE, SSD_GW), F32),
    ]
    return pl.pallas_call(
        _mixer_kernel,
        out_shape=jax.ShapeDtypeStruct(h.shape, F32),
        grid=(bsz, seq // CHUNK),
        in_specs=in_specs,
        out_specs=x_spec,
        scratch_shapes=scratch,
        compiler_params=pltpu.CompilerParams(
            dimension_semantics=("arbitrary", "arbitrary"),
            vmem_limit_bytes=VMEM_LIMIT_BYTES),
        name="mixer",
    )(h, *params)


def _mlp_call(h2d, ng, wup, wdn, fg, final_norm):
    n_tok = h2d.shape[0]
    x_spec = pl.BlockSpec((MLP_TILE, D_MODEL), lambda i: (i, 0))
    return pl.pallas_call(
        functools.partial(_mlp_kernel, final_norm=final_norm),
        out_shape=jax.ShapeDtypeStruct(h2d.shape, F32),
        grid=(n_tok // MLP_TILE,),
        in_specs=[x_spec, _const_spec(ng.shape), _const_spec(wup.shape, True),
                  _const_spec(wdn.shape, True), _const_spec(fg.shape)],
        out_specs=x_spec,
        scratch_shapes=[pltpu.VMEM((MLP_TILE, D_MODEL), BF16)],
        compiler_params=pltpu.CompilerParams(
            dimension_semantics=("arbitrary",),
            vmem_limit_bytes=VMEM_LIMIT_BYTES),
        name="mlp",
    )(h2d, ng, wup, wdn, fg)


def _row(v):
    return v.reshape(1, -1).astype(F32)


def _row8(v):
    return jnp.broadcast_to(v.reshape(1, -1).astype(F32), (SUBLANES, v.size))


def _pad_lanes(v):
    return jnp.pad(v.astype(F32), ((0, 0), (0, LANES - v.shape[-1])))


def kernel(x, norm_mix_g, w_in, b_gate, gmlp_ln_g, gmlp_ln_b, gmlp_w_s, gmlp_b_s, lru_conv_w, lru_conv_b, lru_w_r, lru_b_r, lru_w_i, lru_b_i, lru_lambda, ssd_conv_w, ssd_conv_b, ssd_dt_bias, ssd_a_log, ssd_d, ssd_norm_g, w_branch_a, w_branch_b, w_branch_c, w_out, norm_mlp_g, w_mlp_up, w_mlp_down, final_norm_g):
    bsz, seq, d = x.shape
    assert d == D_MODEL and seq % CHUNK == 0 and (bsz * seq) % MLP_TILE == 0
    depth = w_in.shape[0]
    c_a, c_b, c_z, c_xbc = 2 * D_MODEL, 2 * D_MODEL, D_MODEL, 2 * D_MODEL
    o_b = c_a
    o_z = o_b + c_b
    o_xbc = o_z + c_z
    o_dt = o_xbc + c_xbc
    o_g = o_dt + SSD_HEADS
    h = x
    for l in range(depth):
        wl = w_in[l]
        params = (
            _row8(norm_mix_g[l]),
            wl[:, 0:o_b].astype(BF16),
            wl[:, o_b:o_z].astype(BF16),
            wl[:, o_z:o_xbc].astype(BF16),
            wl[:, o_xbc:o_dt].astype(BF16),
            _pad_lanes(wl[:, o_dt:o_g]).astype(BF16),
            wl[:, o_g:].astype(BF16),
            _row8(b_gate[l]),
            _row8(gmlp_ln_g[l]),
            _row8(gmlp_ln_b[l]),
            gmlp_w_s[l],
            jnp.repeat(gmlp_b_s[l].T, LANES, axis=1),
            lru_conv_w[l],
            _row(lru_conv_b[l]),
            jnp.concatenate([lru_w_r[l], lru_w_i[l]], axis=-1).astype(BF16),
            _row8(lru_b_r[l]),
            _row8(lru_b_i[l]),
            _row8(lru_lambda[l]),
            ssd_conv_w[l],
            _row(ssd_conv_b[l]),
            _pad_lanes(_row(ssd_dt_bias[l])),
            _pad_lanes(_row(ssd_a_log[l])),
            _row8(jnp.repeat(ssd_d[l], SSD_HEAD_DIM)),
            _row8(ssd_norm_g[l]),
            w_branch_a[l].astype(BF16),
            w_branch_b[l].astype(BF16),
            w_branch_c[l].astype(BF16),
            w_out[l].astype(BF16),
        )
        h = _mixer_call(h, params)
        h2d = _mlp_call(h.reshape(bsz * seq, d), _row8(norm_mlp_g[l]),
                        w_mlp_up[l].astype(BF16), w_mlp_down[l].astype(BF16),
                        _row(final_norm_g), final_norm=(l == depth - 1))
        h = h2d.reshape(bsz, seq, d)
    return h
```

```python
import functools

import jax
import jax.numpy as jnp
from jax import lax
from jax.experimental import pallas as pl
from jax.experimental.pallas import tpu as pltpu

F32 = jnp.float32
BF16 = jnp.bfloat16

LANES = 128
SUBLANES = 8
VMEM_LIMIT_BYTES = 60 * 1024 * 1024

EPS = 1e-6
D_MODEL = 1024
CONV_WIDTH = 4
CHUNK = 128
GMLP_GROUPS = 8
LRU_HEADS = 8
LRU_C = 8.0
SSD_HEADS = 16
SSD_HEAD_DIM = 64
SSD_GROUPS = 4
SSD_HPG = SSD_HEADS // SSD_GROUPS
SSD_STATE = 128
SSD_GW = SSD_HPG * SSD_HEAD_DIM
MLP_HIDDEN = 4 * D_MODEL
CONV_COLS = 256
DOT_COLS = 512
COL_U = 0
COL_V = COL_U + D_MODEL
COL_XB = COL_V + D_MODEL
COL_GATE = COL_XB + D_MODEL
COL_Z = COL_GATE + D_MODEL
COL_XBC = COL_Z + D_MODEL
COL_DT = COL_XBC + 2 * D_MODEL
COL_G = COL_DT + LANES
D_IN_PAD = COL_G + 3 * D_MODEL
MLP_TILE = 512
MLP_HCHUNK = 1024


def _dot(a, b):
    return jnp.dot(a, b, preferred_element_type=F32)


def _wdot(a, w_packed):
    return _dot(a, pltpu.bitcast(w_packed, BF16))


def _pack_rows(w):
    bits = lax.bitcast_convert_type(w.astype(BF16), jnp.uint16).astype(jnp.uint32)
    return bits[..., 0::2, :] | (bits[..., 1::2, :] << 16)


def _rows(i, n):
    return slice(i * n, (i + 1) * n)


class _Plan:
    def __init__(self):
        self.steps = []

    def add(self, fn, mxu=0, valu=0, deps=()):
        self.steps.append((fn, mxu, valu, tuple(deps)))
        return len(self.steps) - 1

    def run(self):
        n = len(self.steps)
        users = [[] for _ in range(n)]
        for i, (_, _, _, deps) in enumerate(self.steps):
            for d in deps:
                users[d].append(i)
        tail = [0.0] * n
        for i in reversed(range(n)):
            _, m, v, _ = self.steps[i]
            tail[i] = m + v + max((tail[u] for u in users[i]), default=0.0)
        finish = [None] * n
        clock = {"m": 0.0, "v": 0.0}
        pending = set(range(n))
        while pending:
            best = None
            for i in pending:
                _, m, v, deps = self.steps[i]
                if any(finish[d] is None for d in deps):
                    continue
                start = max([finish[d] for d in deps] + [clock["m"] if m else 0.0, clock["v"] if v else 0.0])
                key = (start, -tail[i], i)
                if best is None or key < best[0]:
                    best = (key, i, start)
            _, i, start = best
            fn, m, v, _ = self.steps[i]
            fn()
            if m:
                clock["m"] = start + m
            if v:
                clock["v"] = start + v
            finish[i] = start + max(m, v)
            pending.remove(i)


def _unrolled(n, body, carry):
    for i in range(n):
        carry = body(i, carry)
    return carry


def _shift_rows(x, d, fill):
    row = lax.broadcasted_iota(jnp.int32, x.shape, 0)
    return jnp.where(row >= d, pltpu.roll(x, d, 0), fill)


def _causal_conv_cols(buf_ref, cols, w_ref, b_ref):
    blk = buf_ref[:, cols]
    acc = blk[SUBLANES:] * w_ref[CONV_WIDTH - 1:CONV_WIDTH, cols] + b_ref[:, cols]
    for k in range(1, CONV_WIDTH):
        tap = pltpu.roll(blk, k, 0)[SUBLANES:]
        acc = acc + tap * w_ref[CONV_WIDTH - 1 - k:CONV_WIDTH - k, cols]
    return acc


def _mixer_kernel(x_ref, ng_ref, win_ref,
                  bgate_ref, lng_ref, lnb_ref, ws_ref, bsfull_ref,
                  lcw_ref, lcb_ref, wri_ref, br_ref, bi_ref, lam_ref,
                  scw_ref, scb_ref, dtb_ref, alog_ref, dskip_ref, sng_ref,
                  wba_ref, wbb_ref, wbc_ref, wo_ref,
                  o_ref,
                  hn_s, za_s, v_s, ya_s, xb_s, gate_s, xc_s, ri_s, yb_s,
                  z_s, xbc_s, xbcc_s, dt_s, y_s, yc_s, graw_s, p_s, mg_s,
                  lruh_s, state_s):
    n_blk8 = CHUNK // SUBLANES
    n_blk16 = CHUNK // (2 * SUBLANES)

    @pl.when(pl.program_id(1) == 0)
    def _reset_state():
        lruh_s[...] = jnp.zeros_like(lruh_s)
        state_s[...] = jnp.zeros_like(state_s)
        xb_s[0:SUBLANES, :] = jnp.zeros((SUBLANES, xb_s.shape[1]), F32)
        xbc_s[0:SUBLANES, :] = jnp.zeros((SUBLANES, xbc_s.shape[1]), F32)


    def dots(dst_ref, row0, lhs_ref, w_ref, n_cols, w_col0=0, dst_col0=0):
        out = []
        for c0 in range(0, n_cols, DOT_COLS):
            n = min(DOT_COLS, n_cols - c0)

            def piece(c0=c0, n=n):
                dst_ref[row0:row0 + CHUNK, dst_col0 + c0:dst_col0 + c0 + n] = _wdot(
                    lhs_ref[...], w_ref[:, w_col0 + c0:w_col0 + c0 + n])
            out.append((n, piece))
        return out

    def norm_step(i):
        parts = []
        for half in range(2):
            x = x_ref[0, _rows(2 * i + half, SUBLANES), :]
            ms = jnp.mean(x * x, axis=-1, keepdims=True)
            parts.append(x * lax.rsqrt(ms + EPS) * ng_ref[...])
        hn_s[_rows(i, 2 * SUBLANES), :] = jnp.concatenate(parts, axis=0).astype(BF16)

    def gmlp_v_step(i):
        rows = _rows(i, SUBLANES)
        v = jax.nn.gelu(za_s[rows, D_MODEL:2 * D_MODEL])
        mu = jnp.mean(v, axis=-1, keepdims=True)
        vc = v - mu
        var = jnp.mean(vc * vc, axis=-1, keepdims=True)
        v_s[rows, :] = vc * lax.rsqrt(var + EPS) * lng_ref[...] + lnb_ref[...]

    def gmlp_u_step(i):
        rows = _rows(i, SUBLANES)
        za_s[rows, 0:D_MODEL] = jax.nn.gelu(za_s[rows, 0:D_MODEL])

    def gmlp_mix_step(g):
        tril = (lax.broadcasted_iota(jnp.int32, (CHUNK, CHUNK), 0)
                >= lax.broadcasted_iota(jnp.int32, (CHUNK, CHUNK), 1))
        cols = slice(g * LANES, (g + 1) * LANES)
        w = jnp.where(tril, ws_ref[g], 0.0).astype(BF16)
        mixed = _dot(w, v_s[:, cols].astype(BF16)) + bsfull_ref[:, cols]
        ya_s[:, cols] = (za_s[:, cols] * mixed).astype(BF16)

    def lru_conv_step(q):
        cols = slice(q * CONV_COLS, (q + 1) * CONV_COLS)
        xc_s[:, cols] = _causal_conv_cols(xb_s, cols, lcw_ref, lcb_ref)

    def lru_history_step():
        xb_s[0:SUBLANES, :] = xb_s[CHUNK:CHUNK + SUBLANES, :]

    def ri_step(h):
        cols = slice(h * LANES, (h + 1) * LANES)
        ri_s[:, 2 * h * LANES:2 * (h + 1) * LANES] = _wdot(xc_s[:, cols].astype(BF16), wri_ref[h])

    lru_carry = [None]

    def lru_scan_block(i, h_prev):
        rows = _rows(i, SUBLANES)
        ri = ri_s[rows, :]
        r_pre = jnp.concatenate(
            [ri[:, 2 * h * LANES:(2 * h + 1) * LANES] for h in range(LRU_HEADS)], axis=1)
        i_pre = jnp.concatenate(
            [ri[:, (2 * h + 1) * LANES:(2 * h + 2) * LANES] for h in range(LRU_HEADS)], axis=1)
        r = jax.nn.sigmoid(r_pre + br_ref[...])
        ig = jax.nn.sigmoid(i_pre + bi_ref[...])
        xc = xc_s[rows, :]
        log_a = r * (-LRU_C * jax.nn.softplus(-lam_ref[...]))
        a_cum = jnp.exp(log_a)
        b_cum = jnp.sqrt(-jnp.tanh(log_a) * (a_cum * a_cum + 1.0)) * (ig * xc)
        for d in (1, 2, 4):
            b_cum = a_cum * _shift_rows(b_cum, d, 0.0) + b_cum
            a_cum = a_cum * _shift_rows(a_cum, d, 1.0)
        hseq = a_cum * h_prev + b_cum
        y = jax.nn.gelu(gate_s[rows, :]) * hseq
        return y, jnp.broadcast_to(hseq[SUBLANES - 1:SUBLANES, :], hseq.shape)

    def lru_scan_step(i):
        if lru_carry[0] is None:
            lru_carry[0] = lruh_s[...]
        y0, h_mid = lru_scan_block(2 * i, lru_carry[0])
        y1, lru_carry[0] = lru_scan_block(2 * i + 1, h_mid)
        yb_s[_rows(i, 2 * SUBLANES), :] = jnp.concatenate([y0, y1], axis=0).astype(BF16)
        if i == n_blk16 - 1:
            lruh_s[...] = lru_carry[0]

    def ssd_conv_step(q):
        cols = slice(q * CONV_COLS, (q + 1) * CONV_COLS)
        xbcc_s[:, cols] = jax.nn.silu(_causal_conv_cols(xbc_s, cols, scw_ref, scb_ref))

    def ssd_history_step():
        xbc_s[0:SUBLANES, :] = xbc_s[CHUNK:CHUNK + SUBLANES, :]

    pre = {}

    def ssd_prelude_step():
        dt = jax.nn.softplus(dt_s[...] + dtb_ref[...])
        cs = dt * (-jnp.exp(alog_ref[...]))
        for d in (1, 2, 4, 8, 16, 32, 64):
            cs = cs + _shift_rows(cs, d, 0.0)
        cs_t = cs.T
        dt_t = dt.T
        pre.update(
            cs=cs, cs_t=cs_t, dt_t=dt_t,
            e_cs=jnp.exp(cs),
            w_t=dt_t * jnp.exp(cs_t[:, CHUNK - 1:CHUNK] - cs_t))

    def ssd_group_step(g):
        cs, cs_t, dt_t, e_cs, w_t = pre["cs"], pre["cs_t"], pre["dt_t"], pre["e_cs"], pre["w_t"]
        tril = (lax.broadcasted_iota(jnp.int32, (CHUNK, CHUNK), 0)
                >= lax.broadcasted_iota(jnp.int32, (CHUNK, CHUNK), 1))
        lane_gw = lax.broadcasted_iota(jnp.int32, (CHUNK, SSD_GW), 1)
        lane_row = lax.broadcasted_iota(jnp.int32, (1, SSD_GW), 1)
        xcols = slice(g * SSD_GW, (g + 1) * SSD_GW)
        bcols = slice(D_MODEL + g * SSD_STATE, D_MODEL + (g + 1) * SSD_STATE)
        ccols = slice(D_MODEL + SSD_GROUPS * SSD_STATE + g * SSD_STATE,
                      D_MODEL + SSD_GROUPS * SSD_STATE + (g + 1) * SSD_STATE)
        xg = xbcc_s[:, xcols]
        bg = xbcc_s[:, bcols]
        cg = xbcc_s[:, ccols]
        state = state_s[g]
        cb = lax.dot_general(cg.astype(BF16), bg.astype(BF16), (((1,), (1,)), ((), ())),
                             preferred_element_type=F32)
        bg_t = bg.T
        m_parts, ce_parts, bw_parts, x_parts, st_parts = [], [], [], [], []
        dec_row = jnp.zeros((1, SSD_GW), F32)
        for r in range(SSD_HPG):
            h = g * SSD_HPG + r
            seg = jnp.where(tril, cs[:, h:h + 1] - cs_t[h:h + 1, :], -jnp.inf)
            m_parts.append((cb * jnp.exp(seg) * dt_t[h:h + 1, :]).astype(BF16))
            ce_parts.append((cg * e_cs[:, h:h + 1]).astype(BF16))
            bw_parts.append((bg_t * w_t[h:h + 1, :]).astype(BF16))
            in_head = (lane_gw >= r * SSD_HEAD_DIM) & (lane_gw < (r + 1) * SSD_HEAD_DIM)
            x_parts.append(jnp.where(in_head, xg, 0.0).astype(BF16))
            st_parts.append(jnp.where(in_head, state, 0.0).astype(BF16))
            in_head_row = (lane_row >= r * SSD_HEAD_DIM) & (lane_row < (r + 1) * SSD_HEAD_DIM)
            dec_row = jnp.where(in_head_row, e_cs[CHUNK - 1:CHUNK, h:h + 1], dec_row)
        x_blk = jnp.concatenate(x_parts, axis=0)
        lhs = jnp.concatenate(m_parts + ce_parts, axis=1)
        rhs = jnp.concatenate([x_blk] + st_parts, axis=0)
        y_s[:, xcols] = _dot(lhs, rhs)
        state_s[g] = state * dec_row + _dot(jnp.concatenate(bw_parts, axis=1), x_blk)

    def ssd_out_step(i):
        outs = []
        for half in range(2):
            rows = _rows(2 * i + half, SUBLANES)
            y = y_s[rows, :] + xbcc_s[rows, 0:D_MODEL] * dskip_ref[...]
            y = y * jax.nn.silu(z_s[rows, :])
            parts = []
            for g in range(SSD_GROUPS):
                yg = y[:, g * SSD_GW:(g + 1) * SSD_GW]
                parts.append(yg * lax.rsqrt(jnp.mean(yg * yg, axis=-1, keepdims=True) + EPS))
            outs.append(jnp.concatenate(parts, axis=1) * sng_ref[...])
        yc_s[_rows(i, 2 * SUBLANES), :] = jnp.concatenate(outs, axis=0).astype(BF16)

    def gated(rows, k):
        cols = slice(k * D_MODEL, (k + 1) * D_MODEL)
        return jax.nn.sigmoid(graw_s[rows, cols] + bgate_ref[:, cols]) * p_s[rows, cols]

    def merge01_step(i):
        rows = _rows(i, SUBLANES)
        p_s[rows, 0:D_MODEL] = gated(rows, 0) + gated(rows, 1)

    def merge2_step(i):
        halves = [p_s[_rows(2 * i + half, SUBLANES), 0:D_MODEL] + gated(_rows(2 * i + half, SUBLANES), 2)
                  for half in range(2)]
        mg_s[_rows(i, 2 * SUBLANES), :] = jnp.concatenate(halves, axis=0).astype(BF16)

    def out_step(q):
        cols = slice(q * DOT_COLS, (q + 1) * DOT_COLS)
        o_ref[0, :, cols] = x_ref[0, :, cols] + _wdot(mg_s[...], wo_ref[:, cols])

    plan = _Plan()

    def add_each(fn, n, deps, mxu=0, valu=0):
        return [plan.add(functools.partial(fn, i), mxu, valu, deps) for i in range(n)]

    def add_dots(dst_ref, row0, lhs_ref, w_ref, n_cols, deps, w_col0=0, dst_col0=0):
        return [plan.add(fn, mxu=n / 2, deps=deps)
                for n, fn in dots(dst_ref, row0, lhs_ref, w_ref, n_cols, w_col0, dst_col0)]

    def piece_of(ids, col):
        return [ids[col // DOT_COLS]]

    norm = add_each(norm_step, n_blk16, (), valu=30)
    d_v = add_dots(za_s, 0, hn_s, win_ref, D_MODEL, norm, w_col0=COL_V, dst_col0=D_MODEL)
    d_u = add_dots(za_s, 0, hn_s, win_ref, D_MODEL, norm, w_col0=COL_U)
    d_xb = add_dots(xb_s, SUBLANES, hn_s, win_ref, D_MODEL, norm, w_col0=COL_XB)
    d_xbc = add_dots(xbc_s, SUBLANES, hn_s, win_ref, 2 * D_MODEL, norm, w_col0=COL_XBC)
    d_dt = add_dots(dt_s, 0, hn_s, win_ref, LANES, norm, w_col0=COL_DT)
    d_gate = add_dots(gate_s, 0, hn_s, win_ref, D_MODEL, norm, w_col0=COL_GATE)
    d_z = add_dots(z_s, 0, hn_s, win_ref, D_MODEL, norm, w_col0=COL_Z)
    d_g = add_dots(graw_s, 0, hn_s, win_ref, 3 * D_MODEL, norm, w_col0=COL_G)

    gv = add_each(gmlp_v_step, n_blk8, d_v, valu=70)
    gu = add_each(gmlp_u_step, n_blk8, d_u, valu=30)
    mix = add_each(gmlp_mix_step, GMLP_GROUPS, gv + gu, mxu=32, valu=40)
    p_a = add_dots(p_s, 0, ya_s, wba_ref, D_MODEL, mix)

    lconv = [plan.add(functools.partial(lru_conv_step, q), valu=110, deps=piece_of(d_xb, q * CONV_COLS))
             for q in range(D_MODEL // CONV_COLS)]
    plan.add(lru_history_step, valu=2, deps=lconv)
    ri = [plan.add(functools.partial(ri_step, h), mxu=32, valu=4, deps=[lconv[h * LANES // CONV_COLS]])
          for h in range(LRU_HEADS)]
    scan = []
    for i in range(n_blk16):
        scan.append(plan.add(functools.partial(lru_scan_step, i), valu=220, deps=ri + d_gate + scan[-1:]))
    p_b = add_dots(p_s, 0, yb_s, wbb_ref, D_MODEL, scan, dst_col0=D_MODEL)

    sconv = [plan.add(functools.partial(ssd_conv_step, q), valu=160, deps=piece_of(d_xbc, q * CONV_COLS))
             for q in range(2 * D_MODEL // CONV_COLS)]
    plan.add(ssd_history_step, valu=4, deps=sconv)
    prelude = plan.add(ssd_prelude_step, valu=100, deps=d_dt)
    grp = add_each(ssd_group_step, SSD_GROUPS, sconv + [prelude], mxu=220, valu=250)
    sout = add_each(ssd_out_step, n_blk16, grp + d_z, valu=80)
    p_c = add_dots(p_s, 0, yc_s, wbc_ref, D_MODEL, sout, dst_col0=2 * D_MODEL)

    m01 = add_each(merge01_step, n_blk8, p_a + p_b + d_g, valu=27)
    m2 = add_each(merge2_step, n_blk16, m01 + p_c, valu=25)
    add_each(out_step, D_MODEL // DOT_COLS, m2, mxu=DOT_COLS / 2, valu=16)
    plan.run()


def _mlp_kernel(x_ref, ng_ref, wup_ref, wdn_ref, fg_ref, o_ref, hn_s, *, final_norm):
    n_blk16 = MLP_TILE // (2 * SUBLANES)

    def norm_body(i, c):
        parts = []
        for half in range(2):
            x = x_ref[_rows(2 * i + half, SUBLANES), :]
            ms = jnp.mean(x * x, axis=-1, keepdims=True)
            parts.append(x * lax.rsqrt(ms + EPS) * ng_ref[...])
        hn_s[_rows(i, 2 * SUBLANES), :] = jnp.concatenate(parts, axis=0).astype(BF16)
        return c
    _unrolled(n_blk16, norm_body, 0)
    hn = hn_s[...]
    acc = x_ref[...]
    for c in range(MLP_HIDDEN // MLP_HCHUNK):
        cols = slice(c * MLP_HCHUNK, (c + 1) * MLP_HCHUNK)
        up = jnp.maximum(_wdot(hn, wup_ref[:, cols]), 0.0)
        acc = acc + _wdot((up * up).astype(BF16), wdn_ref[c * MLP_HCHUNK // 2:(c + 1) * MLP_HCHUNK // 2, :])
    if final_norm:
        ms = jnp.mean(acc * acc, axis=-1, keepdims=True)
        acc = acc * lax.rsqrt(ms + EPS) * fg_ref[...]
    o_ref[...] = acc


def _const_spec(shape, single_buffer=False):
    index_map = lambda *_: (0,) * len(shape)
    if single_buffer:
        return pl.BlockSpec(shape, index_map, pipeline_mode=pl.Buffered(1))
    return pl.BlockSpec(shape, index_map)


def _mixer_call(h, params):
    bsz, seq, _ = h.shape
    x_spec = pl.BlockSpec((1, CHUNK, D_MODEL), lambda b, t: (b, t, 0))
    in_specs = [x_spec] + [_const_spec(p.shape, single_buffer=True) for p in params]
    scratch = [
        pltpu.VMEM((CHUNK, D_MODEL), BF16),
        pltpu.VMEM((CHUNK, 2 * D_MODEL), F32),
        pltpu.VMEM((CHUNK, D_MODEL), F32),
        pltpu.VMEM((CHUNK, D_MODEL), BF16),
        pltpu.VMEM((CHUNK + SUBLANES, D_MODEL), F32),
        pltpu.VMEM((CHUNK, D_MODEL), F32),
        pltpu.VMEM((CHUNK, D_MODEL), F32),
        pltpu.VMEM((CHUNK, 2 * D_MODEL), F32),
        pltpu.VMEM((CHUNK, D_MODEL), BF16),
        pltpu.VMEM((CHUNK, D_MODEL), F32),
        pltpu.VMEM((CHUNK + SUBLANES, 2 * D_MODEL), F32),
        pltpu.VMEM((CHUNK, 2 * D_MODEL), F32),
        pltpu.VMEM((CHUNK, LANES), F32),
        pltpu.VMEM((CHUNK, D_MODEL), F32),
        pltpu.VMEM((CHUNK, D_MODEL), BF16),
        pltpu.VMEM((CHUNK, 3 * D_MODEL), F32),
        pltpu.VMEM((CHUNK, 3 * D_MODEL), F32),
        pltpu.VMEM((CHUNK, D_MODEL), BF16),
        pltpu.VMEM((SUBLANES, D_MODEL), F32),
        pltpu.VMEM((SSD_GROUPS, SSD_STATE, SSD_GW), F32),
    ]
    return pl.pallas_call(
        _mixer_kernel,
        out_shape=jax.ShapeDtypeStruct(h.shape, F32),
        grid=(bsz, seq // CHUNK),
        in_specs=in_specs,
        out_specs=x_spec,
        scratch_shapes=scratch,
        compiler_params=pltpu.CompilerParams(
            dimension_semantics=("arbitrary", "arbitrary"),
            vmem_limit_bytes=VMEM_LIMIT_BYTES),
        name="mixer",
    )(h, *params)


def _mlp_call(h2d, ng, wup, wdn, fg, final_norm):
    n_tok = h2d.shape[0]
    x_spec = pl.BlockSpec((MLP_TILE, D_MODEL), lambda i: (i, 0))
    return pl.pallas_call(
        functools.partial(_mlp_kernel, final_norm=final_norm),
        out_shape=jax.ShapeDtypeStruct(h2d.shape, F32),
        grid=(n_tok // MLP_TILE,),
        in_specs=[x_spec, _const_spec(ng.shape), _const_spec(wup.shape, True),
                  _const_spec(wdn.shape, True), _const_spec(fg.shape)],
        out_specs=x_spec,
        scratch_shapes=[pltpu.VMEM((MLP_TILE, D_MODEL), BF16)],
        compiler_params=pltpu.CompilerParams(
            dimension_semantics=("arbitrary",),
            vmem_limit_bytes=VMEM_LIMIT_BYTES),
        name="mlp",
    )(h2d, ng, wup, wdn, fg)


def _row(v):
    return v.reshape(1, -1).astype(F32)


def _row8(v):
    return jnp.broadcast_to(v.reshape(1, -1).astype(F32), (SUBLANES, v.size))


def _pad_lanes(v):
    return jnp.pad(v.astype(F32), ((0, 0), (0, LANES - v.shape[-1])))


def kernel(x, norm_mix_g, w_in, b_gate, gmlp_ln_g, gmlp_ln_b, gmlp_w_s, gmlp_b_s, lru_conv_w, lru_conv_b, lru_w_r, lru_b_r, lru_w_i, lru_b_i, lru_lambda, ssd_conv_w, ssd_conv_b, ssd_dt_bias, ssd_a_log, ssd_d, ssd_norm_g, w_branch_a, w_branch_b, w_branch_c, w_out, norm_mlp_g, w_mlp_up, w_mlp_down, final_norm_g):
    bsz, seq, d = x.shape
    assert d == D_MODEL and seq % CHUNK == 0 and (bsz * seq) % MLP_TILE == 0
    depth = w_in.shape[0]
    o_dt = COL_DT
    o_g = o_dt + SSD_HEADS
    assert w_in.shape[-1] + LANES - SSD_HEADS == D_IN_PAD
    h = x
    for l in range(depth):
        wl = w_in[l]
        params = (
            _row8(norm_mix_g[l]),
            _pack_rows(jnp.concatenate(
                [wl[:, :o_dt], _pad_lanes(wl[:, o_dt:o_g]), wl[:, o_g:]], axis=1)),
            _row8(b_gate[l]),
            _row8(gmlp_ln_g[l]),
            _row8(gmlp_ln_b[l]),
            gmlp_w_s[l],
            jnp.repeat(gmlp_b_s[l].T, LANES, axis=1),
            lru_conv_w[l],
            _row(lru_conv_b[l]),
            _pack_rows(jnp.concatenate([lru_w_r[l], lru_w_i[l]], axis=-1)),
            _row8(lru_b_r[l]),
            _row8(lru_b_i[l]),
            _row8(lru_lambda[l]),
            ssd_conv_w[l],
            _row(ssd_conv_b[l]),
            _pad_lanes(_row(ssd_dt_bias[l])),
            _pad_lanes(_row(ssd_a_log[l])),
            _row8(jnp.repeat(ssd_d[l], SSD_HEAD_DIM)),
            _row8(ssd_norm_g[l]),
            _pack_rows(w_branch_a[l]),
            _pack_rows(w_branch_b[l]),
            _pack_rows(w_branch_c[l]),
            _pack_rows(w_out[l]),
        )
        h = _mixer_call(h, params)
        h2d = _mlp_call(h.reshape(bsz * seq, d), _row8(norm_mlp_g[l]),
                        _pack_rows(w_mlp_up[l]), _pack_rows(w_mlp_down[l]),
                        _row(final_norm_g), final_norm=(l == depth - 1))
        h = h2d.reshape(bsz, seq, d)
    return h
```

```python
import functools

import jax
import jax.numpy as jnp
from jax import lax
from jax.experimental import pallas as pl
from jax.experimental.pallas import tpu as pltpu

F32 = jnp.float32
BF16 = jnp.bfloat16

LANES = 128
SUBLANES = 8
VMEM_LIMIT_BYTES = 60 * 1024 * 1024

EPS = 1e-6
D_MODEL = 1024
CONV_WIDTH = 4
CHUNK = 128
GMLP_GROUPS = 8
LRU_HEADS = 8
LRU_C = 8.0
SSD_HEADS = 16
SSD_HEAD_DIM = 64
SSD_GROUPS = 4
SSD_HPG = SSD_HEADS // SSD_GROUPS
SSD_STATE = 128
SSD_GW = SSD_HPG * SSD_HEAD_DIM
MLP_HIDDEN = 4 * D_MODEL
CONV_COLS = 256
DOT_COLS = 512
COL_U = 0
COL_V = COL_U + D_MODEL
COL_XB = COL_V + D_MODEL
COL_GATE = COL_XB + D_MODEL
COL_Z = COL_GATE + D_MODEL
COL_XBC = COL_Z + D_MODEL
COL_DT = COL_XBC + 2 * D_MODEL
COL_G = COL_DT + LANES
D_IN_PAD = COL_G + 3 * D_MODEL
MLP_TILE = 512
MLP_HCHUNK = 1024


def _dot(a, b):
    return jnp.dot(a, b, preferred_element_type=F32)


def _rows(i, n):
    return slice(i * n, (i + 1) * n)


class _Plan:
    def __init__(self):
        self.steps = []

    def add(self, fn, mxu=0, valu=0, deps=()):
        self.steps.append((fn, mxu, valu, tuple(deps)))
        return len(self.steps) - 1

    def run(self):
        n = len(self.steps)
        users = [[] for _ in range(n)]
        for i, (_, _, _, deps) in enumerate(self.steps):
            for d in deps:
                users[d].append(i)
        tail = [0.0] * n
        for i in reversed(range(n)):
            _, m, v, _ = self.steps[i]
            tail[i] = m + v + max((tail[u] for u in users[i]), default=0.0)
        finish = [None] * n
        clock = {"m": 0.0, "v": 0.0}
        pending = set(range(n))
        while pending:
            best = None
            for i in pending:
                _, m, v, deps = self.steps[i]
                if any(finish[d] is None for d in deps):
                    continue
                start = max([finish[d] for d in deps] + [clock["m"] if m else 0.0, clock["v"] if v else 0.0])
                key = (start, -tail[i], i)
                if best is None or key < best[0]:
                    best = (key, i, start)
            _, i, start = best
            fn, m, v, _ = self.steps[i]
            fn()
            if m:
                clock["m"] = start + m
            if v:
                clock["v"] = start + v
            finish[i] = start + max(m, v)
            pending.remove(i)


def _unrolled(n, body, carry):
    for i in range(n):
        carry = body(i, carry)
    return carry


def _shift_rows(x, d, fill):
    row = lax.broadcasted_iota(jnp.int32, x.shape, 0)
    return jnp.where(row >= d, pltpu.roll(x, d, 0), fill)


def _causal_conv_cols(buf_ref, cols, w_ref, b_ref):
    blk = buf_ref[:, cols]
    acc = blk[SUBLANES:] * w_ref[CONV_WIDTH - 1:CONV_WIDTH, cols] + b_ref[:, cols]
    for k in range(1, CONV_WIDTH):
        tap = pltpu.roll(blk, k, 0)[SUBLANES:]
        acc = acc + tap * w_ref[CONV_WIDTH - 1 - k:CONV_WIDTH - k, cols]
    return acc


def _mixer_kernel(x_ref, ng_ref, win_ref,
                  bgate_ref, lng_ref, lnb_ref, ws_ref, bsfull_ref,
                  lcw_ref, lcb_ref, wri_ref, br_ref, bi_ref, lam_ref,
                  scw_ref, scb_ref, dtb_ref, alog_ref, dskip_ref, sng_ref,
                  wba_ref, wbb_ref, wbc_ref, wo_ref,
                  o_ref,
                  hn_s, za_s, v_s, ya_s, xb_s, gate_s, xc_s, ri_s, yb_s,
                  z_s, xbc_s, xbcc_s, dt_s, y_s, yc_s, graw_s, p_s, mg_s,
                  lruh_s, state_s):
    n_blk8 = CHUNK // SUBLANES
    n_blk16 = CHUNK // (2 * SUBLANES)

    @pl.when(pl.program_id(1) == 0)
    def _reset_state():
        lruh_s[...] = jnp.zeros_like(lruh_s)
        state_s[...] = jnp.zeros_like(state_s)
        xb_s[0:SUBLANES, :] = jnp.zeros((SUBLANES, xb_s.shape[1]), F32)
        xbc_s[0:SUBLANES, :] = jnp.zeros((SUBLANES, xbc_s.shape[1]), F32)


    def dots(dst_ref, row0, lhs_ref, w_ref, n_cols, w_col0=0, dst_col0=0):
        out = []
        for c0 in range(0, n_cols, DOT_COLS):
            n = min(DOT_COLS, n_cols - c0)

            def piece(c0=c0, n=n):
                dst_ref[row0:row0 + CHUNK, dst_col0 + c0:dst_col0 + c0 + n] = _dot(
                    lhs_ref[...], w_ref[:, w_col0 + c0:w_col0 + c0 + n])
            out.append((n, piece))
        return out

    def norm_step(i):
        parts = []
        for half in range(2):
            x = x_ref[0, _rows(2 * i + half, SUBLANES), :]
            ms = jnp.mean(x * x, axis=-1, keepdims=True)
            parts.append(x * lax.rsqrt(ms + EPS) * ng_ref[...])
        hn_s[_rows(i, 2 * SUBLANES), :] = jnp.concatenate(parts, axis=0).astype(BF16)

    def gmlp_v_step(i):
        rows = _rows(i, SUBLANES)
        v = jax.nn.gelu(za_s[rows, D_MODEL:2 * D_MODEL])
        mu = jnp.mean(v, axis=-1, keepdims=True)
        vc = v - mu
        var = jnp.mean(vc * vc, axis=-1, keepdims=True)
        v_s[rows, :] = vc * lax.rsqrt(var + EPS) * lng_ref[...] + lnb_ref[...]

    def gmlp_u_step(i):
        rows = _rows(i, SUBLANES)
        za_s[rows, 0:D_MODEL] = jax.nn.gelu(za_s[rows, 0:D_MODEL])

    def gmlp_mix_step(g):
        tril = (lax.broadcasted_iota(jnp.int32, (CHUNK, CHUNK), 0)
                >= lax.broadcasted_iota(jnp.int32, (CHUNK, CHUNK), 1))
        cols = slice(g * LANES, (g + 1) * LANES)
        w = jnp.where(tril, ws_ref[g], 0.0).astype(BF16)
        mixed = _dot(w, v_s[:, cols].astype(BF16)) + bsfull_ref[:, cols]
        ya_s[:, cols] = (za_s[:, cols] * mixed).astype(BF16)

    def lru_conv_step(q):
        cols = slice(q * CONV_COLS, (q + 1) * CONV_COLS)
        xc_s[:, cols] = _causal_conv_cols(xb_s, cols, lcw_ref, lcb_ref)

    def lru_history_step():
        xb_s[0:SUBLANES, :] = xb_s[CHUNK:CHUNK + SUBLANES, :]

    def ri_step(h):
        cols = slice(h * LANES, (h + 1) * LANES)
        ri_s[:, 2 * h * LANES:2 * (h + 1) * LANES] = _dot(xc_s[:, cols].astype(BF16), wri_ref[h])

    lru_carry = [None]

    def lru_scan_block(i, h_prev):
        rows = _rows(i, SUBLANES)
        ri = ri_s[rows, :]
        r_pre = jnp.concatenate(
            [ri[:, 2 * h * LANES:(2 * h + 1) * LANES] for h in range(LRU_HEADS)], axis=1)
        i_pre = jnp.concatenate(
            [ri[:, (2 * h + 1) * LANES:(2 * h + 2) * LANES] for h in range(LRU_HEADS)], axis=1)
        r = jax.nn.sigmoid(r_pre + br_ref[...])
        ig = jax.nn.sigmoid(i_pre + bi_ref[...])
        xc = xc_s[rows, :]
        log_a = r * (-LRU_C * jax.nn.softplus(-lam_ref[...]))
        a_cum = jnp.exp(log_a)
        b_cum = jnp.sqrt(-jnp.tanh(log_a) * (a_cum * a_cum + 1.0)) * (ig * xc)
        for d in (1, 2, 4):
            b_cum = a_cum * _shift_rows(b_cum, d, 0.0) + b_cum
            a_cum = a_cum * _shift_rows(a_cum, d, 1.0)
        hseq = a_cum * h_prev + b_cum
        y = jax.nn.gelu(gate_s[rows, :]) * hseq
        return y, jnp.broadcast_to(hseq[SUBLANES - 1:SUBLANES, :], hseq.shape)

    def lru_scan_step(i):
        if lru_carry[0] is None:
            lru_carry[0] = lruh_s[...]
        y0, h_mid = lru_scan_block(2 * i, lru_carry[0])
        y1, lru_carry[0] = lru_scan_block(2 * i + 1, h_mid)
        yb_s[_rows(i, 2 * SUBLANES), :] = jnp.concatenate([y0, y1], axis=0).astype(BF16)
        if i == n_blk16 - 1:
            lruh_s[...] = lru_carry[0]

    def ssd_conv_step(q):
        cols = slice(q * CONV_COLS, (q + 1) * CONV_COLS)
        xbcc_s[:, cols] = jax.nn.silu(_causal_conv_cols(xbc_s, cols, scw_ref, scb_ref))

    def ssd_history_step():
        xbc_s[0:SUBLANES, :] = xbc_s[CHUNK:CHUNK + SUBLANES, :]

    pre = {}

    def ssd_prelude_step():
        dt = jax.nn.softplus(dt_s[...] + dtb_ref[...])
        cs = dt * (-jnp.exp(alog_ref[...]))
        for d in (1, 2, 4, 8, 16, 32, 64):
            cs = cs + _shift_rows(cs, d, 0.0)
        cs_t = cs.T
        dt_t = dt.T
        pre.update(
            cs=cs, cs_t=cs_t, dt_t=dt_t,
            e_cs=jnp.exp(cs),
            w_t=dt_t * jnp.exp(cs_t[:, CHUNK - 1:CHUNK] - cs_t))

    def ssd_group_step(g):
        cs, cs_t, dt_t, e_cs, w_t = pre["cs"], pre["cs_t"], pre["dt_t"], pre["e_cs"], pre["w_t"]
        tril = (lax.broadcasted_iota(jnp.int32, (CHUNK, CHUNK), 0)
                >= lax.broadcasted_iota(jnp.int32, (CHUNK, CHUNK), 1))
        lane_gw = lax.broadcasted_iota(jnp.int32, (CHUNK, SSD_GW), 1)
        lane_row = lax.broadcasted_iota(jnp.int32, (1, SSD_GW), 1)
        xcols = slice(g * SSD_GW, (g + 1) * SSD_GW)
        bcols = slice(D_MODEL + g * SSD_STATE, D_MODEL + (g + 1) * SSD_STATE)
        ccols = slice(D_MODEL + SSD_GROUPS * SSD_STATE + g * SSD_STATE,
                      D_MODEL + SSD_GROUPS * SSD_STATE + (g + 1) * SSD_STATE)
        xg = xbcc_s[:, xcols]
        bg = xbcc_s[:, bcols]
        cg = xbcc_s[:, ccols]
        state = state_s[g]
        cb = lax.dot_general(cg.astype(BF16), bg.astype(BF16), (((1,), (1,)), ((), ())),
                             preferred_element_type=F32)
        bg_t = bg.T
        m_parts, ce_parts, bw_parts, x_parts, st_parts = [], [], [], [], []
        dec_row = jnp.zeros((1, SSD_GW), F32)
        for r in range(SSD_HPG):
            h = g * SSD_HPG + r
            seg = jnp.where(tril, cs[:, h:h + 1] - cs_t[h:h + 1, :], -jnp.inf)
            m_parts.append((cb * jnp.exp(seg) * dt_t[h:h + 1, :]).astype(BF16))
            ce_parts.append((cg * e_cs[:, h:h + 1]).astype(BF16))
            bw_parts.append((bg_t * w_t[h:h + 1, :]).astype(BF16))
            in_head = (lane_gw >= r * SSD_HEAD_DIM) & (lane_gw < (r + 1) * SSD_HEAD_DIM)
            x_parts.append(jnp.where(in_head, xg, 0.0).astype(BF16))
            st_parts.append(jnp.where(in_head, state, 0.0).astype(BF16))
            in_head_row = (lane_row >= r * SSD_HEAD_DIM) & (lane_row < (r + 1) * SSD_HEAD_DIM)
            dec_row = jnp.where(in_head_row, e_cs[CHUNK - 1:CHUNK, h:h + 1], dec_row)
        x_blk = jnp.concatenate(x_parts, axis=0)
        lhs = jnp.concatenate(m_parts + ce_parts, axis=1)
        rhs = jnp.concatenate([x_blk] + st_parts, axis=0)
        y_s[:, xcols] = _dot(lhs, rhs)
        state_s[g] = state * dec_row + _dot(jnp.concatenate(bw_parts, axis=1), x_blk)

    def ssd_out_step(i):
        outs = []
        for half in range(2):
            rows = _rows(2 * i + half, SUBLANES)
            y = y_s[rows, :] + xbcc_s[rows, 0:D_MODEL] * dskip_ref[...]
            y = y * jax.nn.silu(z_s[rows, :])
            parts = []
            for g in range(SSD_GROUPS):
                yg = y[:, g * SSD_GW:(g + 1) * SSD_GW]
                parts.append(yg * lax.rsqrt(jnp.mean(yg * yg, axis=-1, keepdims=True) + EPS))
            outs.append(jnp.concatenate(parts, axis=1) * sng_ref[...])
        yc_s[_rows(i, 2 * SUBLANES), :] = jnp.concatenate(outs, axis=0).astype(BF16)

    def gated(rows, k):
        cols = slice(k * D_MODEL, (k + 1) * D_MODEL)
        return jax.nn.sigmoid(graw_s[rows, cols] + bgate_ref[:, cols]) * p_s[rows, cols]

    def merge01_step(i):
        rows = _rows(i, SUBLANES)
        p_s[rows, 0:D_MODEL] = gated(rows, 0) + gated(rows, 1)

    def merge2_step(i):
        halves = [p_s[_rows(2 * i + half, SUBLANES), 0:D_MODEL] + gated(_rows(2 * i + half, SUBLANES), 2)
                  for half in range(2)]
        mg_s[_rows(i, 2 * SUBLANES), :] = jnp.concatenate(halves, axis=0).astype(BF16)

    def out_step(q):
        cols = slice(q * DOT_COLS, (q + 1) * DOT_COLS)
        o_ref[0, :, cols] = x_ref[0, :, cols] + _dot(mg_s[...], wo_ref[:, cols])

    plan = _Plan()

    def add_each(fn, n, deps, mxu=0, valu=0):
        return [plan.add(functools.partial(fn, i), mxu, valu, deps) for i in range(n)]

    def add_dots(dst_ref, row0, lhs_ref, w_ref, n_cols, deps, w_col0=0, dst_col0=0):
        return [plan.add(fn, mxu=n / 2, deps=deps)
                for n, fn in dots(dst_ref, row0, lhs_ref, w_ref, n_cols, w_col0, dst_col0)]

    def piece_of(ids, col):
        return [ids[col // DOT_COLS]]

    norm = add_each(norm_step, n_blk16, (), valu=30)
    d_v = add_dots(za_s, 0, hn_s, win_ref, D_MODEL, norm, w_col0=COL_V, dst_col0=D_MODEL)
    d_u = add_dots(za_s, 0, hn_s, win_ref, D_MODEL, norm, w_col0=COL_U)
    d_xb = add_dots(xb_s, SUBLANES, hn_s, win_ref, D_MODEL, norm, w_col0=COL_XB)
    d_xbc = add_dots(xbc_s, SUBLANES, hn_s, win_ref, 2 * D_MODEL, norm, w_col0=COL_XBC)
    d_dt = add_dots(dt_s, 0, hn_s, win_ref, LANES, norm, w_col0=COL_DT)
    d_gate = add_dots(gate_s, 0, hn_s, win_ref, D_MODEL, norm, w_col0=COL_GATE)
    d_z = add_dots(z_s, 0, hn_s, win_ref, D_MODEL, norm, w_col0=COL_Z)
    d_g = add_dots(graw_s, 0, hn_s, win_ref, 3 * D_MODEL, norm, w_col0=COL_G)

    gv = add_each(gmlp_v_step, n_blk8, d_v, valu=70)
    gu = add_each(gmlp_u_step, n_blk8, d_u, valu=30)
    mix = add_each(gmlp_mix_step, GMLP_GROUPS, gv + gu, mxu=32, valu=40)
    p_a = add_dots(p_s, 0, ya_s, wba_ref, D_MODEL, mix)

    lconv = [plan.add(functools.partial(lru_conv_step, q), valu=110, deps=piece_of(d_xb, q * CONV_COLS))
             for q in range(D_MODEL // CONV_COLS)]
    plan.add(lru_history_step, valu=2, deps=lconv)
    ri = [plan.add(functools.partial(ri_step, h), mxu=32, valu=4, deps=[lconv[h * LANES // CONV_COLS]])
          for h in range(LRU_HEADS)]
    scan = []
    for i in range(n_blk16):
        scan.append(plan.add(functools.partial(lru_scan_step, i), valu=220, deps=ri + d_gate + scan[-1:]))
    p_b = add_dots(p_s, 0, yb_s, wbb_ref, D_MODEL, scan, dst_col0=D_MODEL)

    sconv = [plan.add(functools.partial(ssd_conv_step, q), valu=160, deps=piece_of(d_xbc, q * CONV_COLS))
             for q in range(2 * D_MODEL // CONV_COLS)]
    plan.add(ssd_history_step, valu=4, deps=sconv)
    prelude = plan.add(ssd_prelude_step, valu=100, deps=d_dt)
    grp = add_each(ssd_group_step, SSD_GROUPS, sconv + [prelude], mxu=220, valu=250)
    sout = add_each(ssd_out_step, n_blk16, grp + d_z, valu=80)
    p_c = add_dots(p_s, 0, yc_s, wbc_ref, D_MODEL, sout, dst_col0=2 * D_MODEL)

    m01 = add_each(merge01_step, n_blk8, p_a + p_b + d_g, valu=27)
    m2 = add_each(merge2_step, n_blk16, m01 + p_c, valu=25)
    add_each(out_step, D_MODEL // DOT_COLS, m2, mxu=DOT_COLS / 2, valu=16)
    plan.run()


def _mlp_kernel(x_ref, ng_ref, wup_ref, wdn_ref, fg_ref, o_ref, hn_s, *, final_norm):
    n_blk16 = MLP_TILE // (2 * SUBLANES)

    def norm_body(i, c):
        parts = []
        for half in range(2):
            x = x_ref[_rows(2 * i + half, SUBLANES), :]
            ms = jnp.mean(x * x, axis=-1, keepdims=True)
            parts.append(x * lax.rsqrt(ms + EPS) * ng_ref[...])
        hn_s[_rows(i, 2 * SUBLANES), :] = jnp.concatenate(parts, axis=0).astype(BF16)
        return c
    _unrolled(n_blk16, norm_body, 0)
    hn = hn_s[...]
    acc = x_ref[...]
    for c in range(MLP_HIDDEN // MLP_HCHUNK):
        cols = slice(c * MLP_HCHUNK, (c + 1) * MLP_HCHUNK)
        up = jnp.maximum(_dot(hn, wup_ref[:, cols]), 0.0)
        acc = acc + _dot((up * up).astype(BF16), wdn_ref[cols, :])
    if final_norm:
        ms = jnp.mean(acc * acc, axis=-1, keepdims=True)
        acc = acc * lax.rsqrt(ms + EPS) * fg_ref[...]
    o_ref[...] = acc


def _const_spec(shape):
    return pl.BlockSpec(shape, lambda *_: (0,) * len(shape), pipeline_mode=pl.Buffered(1))


def _layer_spec(shape, layer):
    rest = tuple(shape[1:])
    return pl.BlockSpec((None,) + rest, lambda *_: (layer,) + (0,) * len(rest),
                        pipeline_mode=pl.Buffered(1))


def _param_spec(p, layer):
    return _const_spec(p.shape) if layer is None else _layer_spec(p.shape, layer)


def _mixer_call(h, params):
    bsz, seq, _ = h.shape
    x_spec = pl.BlockSpec((1, CHUNK, D_MODEL), lambda b, t: (b, t, 0))
    in_specs = [x_spec] + [_param_spec(p, layer) for p, layer in params]
    scratch = [
        pltpu.VMEM((CHUNK, D_MODEL), BF16),
        pltpu.VMEM((CHUNK, 2 * D_MODEL), F32),
        pltpu.VMEM((CHUNK, D_MODEL), F32),
        pltpu.VMEM((CHUNK, D_MODEL), BF16),
        pltpu.VMEM((CHUNK + SUBLANES, D_MODEL), F32),
        pltpu.VMEM((CHUNK, D_MODEL), F32),
        pltpu.VMEM((CHUNK, D_MODEL), F32),
        pltpu.VMEM((CHUNK, 2 * D_MODEL), F32),
        pltpu.VMEM((CHUNK, D_MODEL), BF16),
        pltpu.VMEM((CHUNK, D_MODEL), F32),
        pltpu.VMEM((CHUNK + SUBLANES, 2 * D_MODEL), F32),
        pltpu.VMEM((CHUNK, 2 * D_MODEL), F32),
        pltpu.VMEM((CHUNK, LANES), F32),
        pltpu.VMEM((CHUNK, D_MODEL), F32),
        pltpu.VMEM((CHUNK, D_MODEL), BF16),
        pltpu.VMEM((CHUNK, 3 * D_MODEL), F32),
        pltpu.VMEM((CHUNK, 3 * D_MODEL), F32),
        pltpu.VMEM((CHUNK, D_MODEL), BF16),
        pltpu.VMEM((SUBLANES, D_MODEL), F32),
        pltpu.VMEM((SSD_GROUPS, SSD_STATE, SSD_GW), F32),
    ]
    return pl.pallas_call(
        _mixer_kernel,
        out_shape=jax.ShapeDtypeStruct(h.shape, F32),
        grid=(bsz, seq // CHUNK),
        in_specs=in_specs,
        out_specs=x_spec,
        scratch_shapes=scratch,
        compiler_params=pltpu.CompilerParams(
            dimension_semantics=("arbitrary", "arbitrary"),
            vmem_limit_bytes=VMEM_LIMIT_BYTES),
        name="mixer",
    )(h, *[p for p, _ in params])


def _mlp_call(h2d, ng, wup, wdn, layer, fg, final_norm):
    n_tok = h2d.shape[0]
    x_spec = pl.BlockSpec((MLP_TILE, D_MODEL), lambda i: (i, 0))
    return pl.pallas_call(
        functools.partial(_mlp_kernel, final_norm=final_norm),
        out_shape=jax.ShapeDtypeStruct(h2d.shape, F32),
        grid=(n_tok // MLP_TILE,),
        in_specs=[x_spec, _const_spec(ng.shape), _layer_spec(wup.shape, layer),
                  _layer_spec(wdn.shape, layer), _const_spec(fg.shape)],
        out_specs=x_spec,
        scratch_shapes=[pltpu.VMEM((MLP_TILE, D_MODEL), BF16)],
        compiler_params=pltpu.CompilerParams(
            dimension_semantics=("arbitrary",),
            vmem_limit_bytes=VMEM_LIMIT_BYTES),
        name="mlp",
    )(h2d, ng, wup, wdn, fg)


def _row(v):
    return v.reshape(1, -1).astype(F32)


def _row8(v):
    return jnp.broadcast_to(v.reshape(1, -1).astype(F32), (SUBLANES, v.size))


def _pad_lanes(v):
    return jnp.pad(v.astype(F32), ((0, 0), (0, LANES - v.shape[-1])))


def kernel(x, norm_mix_g, w_in, b_gate, gmlp_ln_g, gmlp_ln_b, gmlp_w_s, gmlp_b_s, lru_conv_w, lru_conv_b, lru_w_r, lru_b_r, lru_w_i, lru_b_i, lru_lambda, ssd_conv_w, ssd_conv_b, ssd_dt_bias, ssd_a_log, ssd_d, ssd_norm_g, w_branch_a, w_branch_b, w_branch_c, w_out, norm_mlp_g, w_mlp_up, w_mlp_down, final_norm_g):
    bsz, seq, d = x.shape
    assert d == D_MODEL and seq % CHUNK == 0 and (bsz * seq) % MLP_TILE == 0
    depth = w_in.shape[0]
    o_dt = COL_DT
    o_g = o_dt + SSD_HEADS
    assert w_in.shape[-1] + LANES - SSD_HEADS == D_IN_PAD
    dt_pad = ((0, 0), (0, 0), (0, LANES - SSD_HEADS))
    w_in_b = jnp.concatenate(
        [w_in[:, :, :o_dt], jnp.pad(w_in[:, :, o_dt:o_g], dt_pad), w_in[:, :, o_g:]], axis=2).astype(BF16)
    w_ri_b = jnp.concatenate([lru_w_r, lru_w_i], axis=-1).astype(BF16)
    w_ba_b, w_bb_b, w_bc_b, w_out_b = (w.astype(BF16) for w in (w_branch_a, w_branch_b, w_branch_c, w_out))
    w_up_b, w_dn_b = w_mlp_up.astype(BF16), w_mlp_down.astype(BF16)
    h = x
    for l in range(depth):
        params = (
            (_row8(norm_mix_g[l]), None),
            (w_in_b, l),
            (_row8(b_gate[l]), None),
            (_row8(gmlp_ln_g[l]), None),
            (_row8(gmlp_ln_b[l]), None),
            (gmlp_w_s, l),
            (jnp.repeat(gmlp_b_s[l].T, LANES, axis=1), None),
            (lru_conv_w, l),
            (_row(lru_conv_b[l]), None),
            (w_ri_b, l),
            (_row8(lru_b_r[l]), None),
            (_row8(lru_b_i[l]), None),
            (_row8(lru_lambda[l]), None),
            (ssd_conv_w, l),
            (_row(ssd_conv_b[l]), None),
            (_pad_lanes(_row(ssd_dt_bias[l])), None),
            (_pad_lanes(_row(ssd_a_log[l])), None),
            (_row8(jnp.repeat(ssd_d[l], SSD_HEAD_DIM)), None),
            (_row8(ssd_norm_g[l]), None),
            (w_ba_b, l),
            (w_bb_b, l),
            (w_bc_b, l),
            (w_out_b, l),
        )
        h = _mixer_call(h, params)
        h2d = _mlp_call(h.reshape(bsz * seq, d), _row8(norm_mlp_g[l]), w_up_b, w_dn_b, l,
                        _row(final_norm_g), final_norm=(l == depth - 1))
        h = h2d.reshape(bsz, seq, d)
    return h
```

```python
import functools

import jax
import jax.numpy as jnp
from jax import lax
from jax.experimental import pallas as pl
from jax.experimental.pallas import tpu as pltpu

F32 = jnp.float32
BF16 = jnp.bfloat16

LANES = 128
SUBLANES = 8
VMEM_LIMIT_BYTES = 60 * 1024 * 1024

EPS = 1e-6
D_MODEL = 1024
CONV_WIDTH = 4
CHUNK = 128
TQ = 256
NCH = TQ // CHUNK
GMLP_GROUPS = 8
LRU_HEADS = 8
LRU_C = 8.0
SSD_HEADS = 16
SSD_HEAD_DIM = 64
SSD_GROUPS = 4
SSD_HPG = SSD_HEADS // SSD_GROUPS
SSD_STATE = 128
SSD_GW = SSD_HPG * SSD_HEAD_DIM
MLP_HIDDEN = 4 * D_MODEL
CONV_COLS = 256
DOT_COLS = 512
COL_U = 0
COL_V = COL_U + D_MODEL
COL_XB = COL_V + D_MODEL
COL_GATE = COL_XB + D_MODEL
COL_Z = COL_GATE + D_MODEL
COL_XBC = COL_Z + D_MODEL
COL_DT = COL_XBC + 2 * D_MODEL
MLP_TILE = 512
MLP_HCHUNK = 1024


def _dot(a, b):
    return jnp.dot(a, b, preferred_element_type=F32)


def _rows(i, n):
    return slice(i * n, (i + 1) * n)


class _Plan:
    def __init__(self):
        self.steps = []

    def add(self, fn, mxu=0, valu=0, deps=()):
        self.steps.append((fn, mxu, valu, tuple(deps)))
        return len(self.steps) - 1

    def run(self):
        n = len(self.steps)
        users = [[] for _ in range(n)]
        for i, (_, _, _, deps) in enumerate(self.steps):
            for d in deps:
                users[d].append(i)
        tail = [0.0] * n
        for i in reversed(range(n)):
            _, m, v, _ = self.steps[i]
            tail[i] = m + v + max((tail[u] for u in users[i]), default=0.0)
        finish = [None] * n
        clock = {"m": 0.0, "v": 0.0}
        pending = set(range(n))
        while pending:
            best = None
            for i in pending:
                _, m, v, deps = self.steps[i]
                if any(finish[d] is None for d in deps):
                    continue
                start = max([finish[d] for d in deps] + [clock["m"] if m else 0.0, clock["v"] if v else 0.0])
                key = (start, -tail[i], i)
                if best is None or key < best[0]:
                    best = (key, i, start)
            _, i, start = best
            fn, m, v, _ = self.steps[i]
            fn()
            if m:
                clock["m"] = start + m
            if v:
                clock["v"] = start + v
            finish[i] = start + max(m, v)
            pending.remove(i)


def _unrolled(n, body, carry):
    for i in range(n):
        carry = body(i, carry)
    return carry


def _shift_rows(x, d, fill):
    row = lax.broadcasted_iota(jnp.int32, x.shape, 0)
    return jnp.where(row >= d, pltpu.roll(x, d, 0), fill)


def _causal_conv_cols(buf_ref, cols, w_ref, b_ref):
    blk = buf_ref[:, cols]
    acc = blk[SUBLANES:] * w_ref[CONV_WIDTH - 1:CONV_WIDTH, cols] + b_ref[:, cols]
    for k in range(1, CONV_WIDTH):
        tap = pltpu.roll(blk, k, 0)[SUBLANES:]
        acc = acc + tap * w_ref[CONV_WIDTH - 1 - k:CONV_WIDTH - k, cols]
    return acc


def _mixer_kernel(x_ref, ng_ref, win_ref, wdt_ref, wg_ref,
                  bgate_ref, lng_ref, lnb_ref, ws_ref, bsfull_ref,
                  lcw_ref, lcb_ref, wri_ref, br_ref, bi_ref, lam_ref,
                  scw_ref, scb_ref, dtb_ref, alog_ref, dskip_ref, sng_ref,
                  wba_ref, wbb_ref, wbc_ref, wo_ref,
                  o_ref,
                  hn_s, za_s, v_s, ya_s, xb_s, gate_s, xc_s, ri_s, yb_s,
                  z_s, xbc_s, xbcc_s, dt_s, y_s, yc_s, graw_s, p_s, mg_s,
                  lruh_s, state_s):
    n_blk8 = TQ // SUBLANES
    n_blk16 = TQ // (2 * SUBLANES)

    @pl.when(pl.program_id(1) == 0)
    def _reset_state():
        lruh_s[...] = jnp.zeros_like(lruh_s)
        state_s[...] = jnp.zeros_like(state_s)
        xb_s[0:SUBLANES, :] = jnp.zeros((SUBLANES, xb_s.shape[1]), F32)
        xbc_s[0:SUBLANES, :] = jnp.zeros((SUBLANES, xbc_s.shape[1]), F32)


    def dots(dst_ref, row0, lhs_ref, w_ref, n_cols, w_col0=0, dst_col0=0):
        out = []
        for c0 in range(0, n_cols, DOT_COLS):
            n = min(DOT_COLS, n_cols - c0)

            def piece(c0=c0, n=n):
                dst_ref[row0:row0 + TQ, dst_col0 + c0:dst_col0 + c0 + n] = _dot(
                    lhs_ref[...], w_ref[:, w_col0 + c0:w_col0 + c0 + n])
            out.append((n, piece))
        return out

    def norm_step(i):
        parts = []
        for half in range(2):
            x = x_ref[0, _rows(2 * i + half, SUBLANES), :]
            ms = jnp.mean(x * x, axis=-1, keepdims=True)
            parts.append(x * lax.rsqrt(ms + EPS) * ng_ref[...])
        hn_s[_rows(i, 2 * SUBLANES), :] = jnp.concatenate(parts, axis=0).astype(BF16)

    def gmlp_v_step(i):
        rows = _rows(i, SUBLANES)
        v = jax.nn.gelu(za_s[rows, D_MODEL:2 * D_MODEL])
        mu = jnp.mean(v, axis=-1, keepdims=True)
        vc = v - mu
        var = jnp.mean(vc * vc, axis=-1, keepdims=True)
        v_s[rows, :] = vc * lax.rsqrt(var + EPS) * lng_ref[...] + lnb_ref[...]

    def gmlp_u_step(i):
        rows = _rows(i, SUBLANES)
        za_s[rows, 0:D_MODEL] = jax.nn.gelu(za_s[rows, 0:D_MODEL])

    def gmlp_mix_step(c, g):
        tril = (lax.broadcasted_iota(jnp.int32, (CHUNK, CHUNK), 0)
                >= lax.broadcasted_iota(jnp.int32, (CHUNK, CHUNK), 1))
        rws = _rows(c, CHUNK)
        cols = slice(g * LANES, (g + 1) * LANES)
        w = jnp.where(tril, ws_ref[g], 0.0).astype(BF16)
        mixed = _dot(w, v_s[rws, cols].astype(BF16)) + bsfull_ref[:, cols]
        ya_s[rws, cols] = (za_s[rws, cols] * mixed).astype(BF16)

    def lru_conv_step(q):
        cols = slice(q * CONV_COLS, (q + 1) * CONV_COLS)
        xc_s[:, cols] = _causal_conv_cols(xb_s, cols, lcw_ref, lcb_ref)

    def lru_history_step():
        xb_s[0:SUBLANES, :] = xb_s[TQ:TQ + SUBLANES, :]

    def ri_step(h):
        cols = slice(h * LANES, (h + 1) * LANES)
        ri_s[:, 2 * h * LANES:2 * (h + 1) * LANES] = _dot(xc_s[:, cols].astype(BF16), wri_ref[h])

    lru_carry = [None]

    def lru_scan_block(i, h_prev):
        rows = _rows(i, SUBLANES)
        ri = ri_s[rows, :]
        r_pre = jnp.concatenate(
            [ri[:, 2 * h * LANES:(2 * h + 1) * LANES] for h in range(LRU_HEADS)], axis=1)
        i_pre = jnp.concatenate(
            [ri[:, (2 * h + 1) * LANES:(2 * h + 2) * LANES] for h in range(LRU_HEADS)], axis=1)
        r = jax.nn.sigmoid(r_pre + br_ref[...])
        ig = jax.nn.sigmoid(i_pre + bi_ref[...])
        xc = xc_s[rows, :]
        log_a = r * (-LRU_C * jax.nn.softplus(-lam_ref[...]))
        a_cum = jnp.exp(log_a)
        b_cum = jnp.sqrt(-jnp.tanh(log_a) * (a_cum * a_cum + 1.0)) * (ig * xc)
        for d in (1, 2, 4):
            b_cum = a_cum * _shift_rows(b_cum, d, 0.0) + b_cum
            a_cum = a_cum * _shift_rows(a_cum, d, 1.0)
        hseq = a_cum * h_prev + b_cum
        y = jax.nn.gelu(gate_s[rows, :]) * hseq
        return y, jnp.broadcast_to(hseq[SUBLANES - 1:SUBLANES, :], hseq.shape)

    def lru_scan_step(i):
        if lru_carry[0] is None:
            lru_carry[0] = lruh_s[...]
        y0, h_mid = lru_scan_block(2 * i, lru_carry[0])
        y1, lru_carry[0] = lru_scan_block(2 * i + 1, h_mid)
        yb_s[_rows(i, 2 * SUBLANES), :] = jnp.concatenate([y0, y1], axis=0).astype(BF16)
        if i == n_blk16 - 1:
            lruh_s[...] = lru_carry[0]

    def ssd_conv_step(q):
        cols = slice(q * CONV_COLS, (q + 1) * CONV_COLS)
        xbcc_s[:, cols] = jax.nn.silu(_causal_conv_cols(xbc_s, cols, scw_ref, scb_ref))

    def ssd_history_step():
        xbc_s[0:SUBLANES, :] = xbc_s[TQ:TQ + SUBLANES, :]

    pres = [dict() for _ in range(NCH)]

    def ssd_prelude_step(c):
        pre = pres[c]
        dt = jax.nn.softplus(dt_s[_rows(c, CHUNK), :] + dtb_ref[...])
        cs = dt * (-jnp.exp(alog_ref[...]))
        for d in (1, 2, 4, 8, 16, 32, 64):
            cs = cs + _shift_rows(cs, d, 0.0)
        cs_t = cs.T
        dt_t = dt.T
        pre.update(
            cs=cs, cs_t=cs_t, dt_t=dt_t,
            e_cs=jnp.exp(cs),
            w_t=dt_t * jnp.exp(cs_t[:, CHUNK - 1:CHUNK] - cs_t))

    def ssd_group_step(c, g):
        pre = pres[c]
        rws = _rows(c, CHUNK)
        cs, cs_t, dt_t, e_cs, w_t = pre["cs"], pre["cs_t"], pre["dt_t"], pre["e_cs"], pre["w_t"]
        tril = (lax.broadcasted_iota(jnp.int32, (CHUNK, CHUNK), 0)
                >= lax.broadcasted_iota(jnp.int32, (CHUNK, CHUNK), 1))
        lane_gw = lax.broadcasted_iota(jnp.int32, (CHUNK, SSD_GW), 1)
        lane_row = lax.broadcasted_iota(jnp.int32, (1, SSD_GW), 1)
        xcols = slice(g * SSD_GW, (g + 1) * SSD_GW)
        bcols = slice(D_MODEL + g * SSD_STATE, D_MODEL + (g + 1) * SSD_STATE)
        ccols = slice(D_MODEL + SSD_GROUPS * SSD_STATE + g * SSD_STATE,
                      D_MODEL + SSD_GROUPS * SSD_STATE + (g + 1) * SSD_STATE)
        xg = xbcc_s[rws, xcols]
        bg = xbcc_s[rws, bcols]
        cg = xbcc_s[rws, ccols]
        state = state_s[g]
        cb = lax.dot_general(cg.astype(BF16), bg.astype(BF16), (((1,), (1,)), ((), ())),
                             preferred_element_type=F32)
        bg_t = bg.T
        m_parts, ce_parts, bw_parts, x_parts, st_parts = [], [], [], [], []
        dec_row = jnp.zeros((1, SSD_GW), F32)
        for r in range(SSD_HPG):
            h = g * SSD_HPG + r
            seg = jnp.where(tril, cs[:, h:h + 1] - cs_t[h:h + 1, :], -jnp.inf)
            m_parts.append((cb * jnp.exp(seg) * dt_t[h:h + 1, :]).astype(BF16))
            ce_parts.append((cg * e_cs[:, h:h + 1]).astype(BF16))
            bw_parts.append((bg_t * w_t[h:h + 1, :]).astype(BF16))
            in_head = (lane_gw >= r * SSD_HEAD_DIM) & (lane_gw < (r + 1) * SSD_HEAD_DIM)
            x_parts.append(jnp.where(in_head, xg, 0.0).astype(BF16))
            st_parts.append(jnp.where(in_head, state, 0.0).astype(BF16))
            in_head_row = (lane_row >= r * SSD_HEAD_DIM) & (lane_row < (r + 1) * SSD_HEAD_DIM)
            dec_row = jnp.where(in_head_row, e_cs[CHUNK - 1:CHUNK, h:h + 1], dec_row)
        x_blk = jnp.concatenate(x_parts, axis=0)
        lhs = jnp.concatenate(m_parts + ce_parts, axis=1)
        rhs = jnp.concatenate([x_blk] + st_parts, axis=0)
        y_s[rws, xcols] = _dot(lhs, rhs)
        state_s[g] = state * dec_row + _dot(jnp.concatenate(bw_parts, axis=1), x_blk)

    def ssd_out_step(i):
        outs = []
        for half in range(2):
            rows = _rows(2 * i + half, SUBLANES)
            y = y_s[rows, :] + xbcc_s[rows, 0:D_MODEL] * dskip_ref[...]
            y = y * jax.nn.silu(z_s[rows, :])
            parts = []
            for g in range(SSD_GROUPS):
                yg = y[:, g * SSD_GW:(g + 1) * SSD_GW]
                parts.append(yg * lax.rsqrt(jnp.mean(yg * yg, axis=-1, keepdims=True) + EPS))
            outs.append(jnp.concatenate(parts, axis=1) * sng_ref[...])
        yc_s[_rows(i, 2 * SUBLANES), :] = jnp.concatenate(outs, axis=0).astype(BF16)

    def gated(rows, k):
        cols = slice(k * D_MODEL, (k + 1) * D_MODEL)
        return jax.nn.sigmoid(graw_s[rows, cols] + bgate_ref[:, cols]) * p_s[rows, cols]

    def merge01_step(i):
        rows = _rows(i, SUBLANES)
        p_s[rows, 0:D_MODEL] = gated(rows, 0) + gated(rows, 1)

    def merge2_step(i):
        halves = [p_s[_rows(2 * i + half, SUBLANES), 0:D_MODEL] + gated(_rows(2 * i + half, SUBLANES), 2)
                  for half in range(2)]
        mg_s[_rows(i, 2 * SUBLANES), :] = jnp.concatenate(halves, axis=0).astype(BF16)

    def out_step(q):
        cols = slice(q * DOT_COLS, (q + 1) * DOT_COLS)
        o_ref[0, :, cols] = x_ref[0, :, cols] + _dot(mg_s[...], wo_ref[:, cols])

    plan = _Plan()

    def add_each(fn, n, deps, mxu=0, valu=0):
        return [plan.add(functools.partial(fn, i), mxu, valu, deps) for i in range(n)]

    def add_dots(dst_ref, row0, lhs_ref, w_ref, n_cols, deps, w_col0=0, dst_col0=0):
        return [plan.add(fn, mxu=NCH * n / 2, deps=deps)
                for n, fn in dots(dst_ref, row0, lhs_ref, w_ref, n_cols, w_col0, dst_col0)]

    def piece_of(ids, col):
        return [ids[col // DOT_COLS]]

    norm = add_each(norm_step, n_blk16, (), valu=30)
    d_v = add_dots(za_s, 0, hn_s, win_ref, D_MODEL, norm, w_col0=COL_V, dst_col0=D_MODEL)
    d_u = add_dots(za_s, 0, hn_s, win_ref, D_MODEL, norm, w_col0=COL_U)
    d_xb = add_dots(xb_s, SUBLANES, hn_s, win_ref, D_MODEL, norm, w_col0=COL_XB)
    d_xbc = add_dots(xbc_s, SUBLANES, hn_s, win_ref, 2 * D_MODEL, norm, w_col0=COL_XBC)
    d_dt = add_dots(dt_s, 0, hn_s, wdt_ref, LANES, norm)
    d_gate = add_dots(gate_s, 0, hn_s, win_ref, D_MODEL, norm, w_col0=COL_GATE)
    d_z = add_dots(z_s, 0, hn_s, win_ref, D_MODEL, norm, w_col0=COL_Z)
    d_g = add_dots(graw_s, 0, hn_s, wg_ref, 3 * D_MODEL, norm)

    gv = add_each(gmlp_v_step, n_blk8, d_v, valu=70)
    gu = add_each(gmlp_u_step, n_blk8, d_u, valu=30)
    mix = [plan.add(functools.partial(gmlp_mix_step, c, g), 32, 40, gv + gu)
           for c in range(NCH) for g in range(GMLP_GROUPS)]
    p_a = add_dots(p_s, 0, ya_s, wba_ref, D_MODEL, mix)

    lconv = [plan.add(functools.partial(lru_conv_step, q), valu=110 * NCH, deps=piece_of(d_xb, q * CONV_COLS))
             for q in range(D_MODEL // CONV_COLS)]
    plan.add(lru_history_step, valu=2, deps=lconv)
    ri = [plan.add(functools.partial(ri_step, h), mxu=32 * NCH, valu=4 * NCH, deps=[lconv[h * LANES // CONV_COLS]])
          for h in range(LRU_HEADS)]
    scan = []
    for i in range(n_blk16):
        scan.append(plan.add(functools.partial(lru_scan_step, i), valu=220, deps=ri + d_gate + scan[-1:]))
    p_b = add_dots(p_s, 0, yb_s, wbb_ref, D_MODEL, scan, dst_col0=D_MODEL)

    sconv = [plan.add(functools.partial(ssd_conv_step, q), valu=160 * NCH, deps=piece_of(d_xbc, q * CONV_COLS))
             for q in range(2 * D_MODEL // CONV_COLS)]
    plan.add(ssd_history_step, valu=4, deps=sconv)
    grp = []
    for c in range(NCH):
        prelude = plan.add(functools.partial(ssd_prelude_step, c), valu=100, deps=d_dt)
        grp += [plan.add(functools.partial(ssd_group_step, c, g), 220, 250,
                         sconv + [prelude] + grp[(c - 1) * SSD_GROUPS + g:(c - 1) * SSD_GROUPS + g + 1] * (c > 0))
                for g in range(SSD_GROUPS)]
    sout = add_each(ssd_out_step, n_blk16, grp + d_z, valu=80)
    p_c = add_dots(p_s, 0, yc_s, wbc_ref, D_MODEL, sout, dst_col0=2 * D_MODEL)

    m01 = add_each(merge01_step, n_blk8, p_a + p_b + d_g, valu=27)
    m2 = add_each(merge2_step, n_blk16, m01 + p_c, valu=25)
    add_each(out_step, D_MODEL // DOT_COLS, m2, mxu=NCH * DOT_COLS / 2, valu=16 * NCH)
    plan.run()


def _mlp_kernel(x_ref, ng_ref, wup_ref, wdn_ref, fg_ref, o_ref, hn_s, *, final_norm):
    n_blk16 = MLP_TILE // (2 * SUBLANES)

    def norm_body(i, c):
        parts = []
        for half in range(2):
            x = x_ref[_rows(2 * i + half, SUBLANES), :]
            ms = jnp.mean(x * x, axis=-1, keepdims=True)
            parts.append(x * lax.rsqrt(ms + EPS) * ng_ref[...])
        hn_s[_rows(i, 2 * SUBLANES), :] = jnp.concatenate(parts, axis=0).astype(BF16)
        return c
    _unrolled(n_blk16, norm_body, 0)
    hn = hn_s[...]
    acc = x_ref[...]
    for c in range(MLP_HIDDEN // MLP_HCHUNK):
        cols = slice(c * MLP_HCHUNK, (c + 1) * MLP_HCHUNK)
        up = jnp.maximum(_dot(hn, wup_ref[:, cols]), 0.0)
        acc = acc + _dot((up * up).astype(BF16), wdn_ref[cols, :])
    if final_norm:
        ms = jnp.mean(acc * acc, axis=-1, keepdims=True)
        acc = acc * lax.rsqrt(ms + EPS) * fg_ref[...]
    o_ref[...] = acc


def _const_spec(shape):
    return pl.BlockSpec(shape, lambda *_: (0,) * len(shape), pipeline_mode=pl.Buffered(1))


def _layer_spec(shape, layer):
    rest = tuple(shape[1:])
    return pl.BlockSpec((None,) + rest, lambda *_: (layer,) + (0,) * len(rest),
                        pipeline_mode=pl.Buffered(1))


def _param_spec(p, layer):
    return _const_spec(p.shape) if layer is None else _layer_spec(p.shape, layer)


def _mixer_call(h, params):
    bsz, seq, _ = h.shape
    x_spec = pl.BlockSpec((1, TQ, D_MODEL), lambda b, t: (b, t, 0))
    in_specs = [x_spec] + [_param_spec(p, layer) for p, layer in params]
    scratch = [
        pltpu.VMEM((TQ, D_MODEL), BF16),
        pltpu.VMEM((TQ, 2 * D_MODEL), F32),
        pltpu.VMEM((TQ, D_MODEL), F32),
        pltpu.VMEM((TQ, D_MODEL), BF16),
        pltpu.VMEM((TQ + SUBLANES, D_MODEL), F32),
        pltpu.VMEM((TQ, D_MODEL), F32),
        pltpu.VMEM((TQ, D_MODEL), F32),
        pltpu.VMEM((TQ, 2 * D_MODEL), F32),
        pltpu.VMEM((TQ, D_MODEL), BF16),
        pltpu.VMEM((TQ, D_MODEL), F32),
        pltpu.VMEM((TQ + SUBLANES, 2 * D_MODEL), F32),
        pltpu.VMEM((TQ, 2 * D_MODEL), F32),
        pltpu.VMEM((TQ, LANES), F32),
        pltpu.VMEM((TQ, D_MODEL), F32),
        pltpu.VMEM((TQ, D_MODEL), BF16),
        pltpu.VMEM((TQ, 3 * D_MODEL), F32),
        pltpu.VMEM((TQ, 3 * D_MODEL), F32),
        pltpu.VMEM((TQ, D_MODEL), BF16),
        pltpu.VMEM((SUBLANES, D_MODEL), F32),
        pltpu.VMEM((SSD_GROUPS, SSD_STATE, SSD_GW), F32),
    ]
    return pl.pallas_call(
        _mixer_kernel,
        out_shape=jax.ShapeDtypeStruct(h.shape, F32),
        grid=(bsz, seq // TQ),
        in_specs=in_specs,
        out_specs=x_spec,
        scratch_shapes=scratch,
        compiler_params=pltpu.CompilerParams(
            dimension_semantics=("arbitrary", "arbitrary"),
            vmem_limit_bytes=VMEM_LIMIT_BYTES),
        name="mixer",
    )(h, *[p for p, _ in params])


def _mlp_call(h2d, ng, wup, wdn, layer, fg, final_norm):
    n_tok = h2d.shape[0]
    x_spec = pl.BlockSpec((MLP_TILE, D_MODEL), lambda i: (i, 0))
    return pl.pallas_call(
        functools.partial(_mlp_kernel, final_norm=final_norm),
        out_shape=jax.ShapeDtypeStruct(h2d.shape, F32),
        grid=(n_tok // MLP_TILE,),
        in_specs=[x_spec, _const_spec(ng.shape), _layer_spec(wup.shape, layer),
                  _layer_spec(wdn.shape, layer), _const_spec(fg.shape)],
        out_specs=x_spec,
        scratch_shapes=[pltpu.VMEM((MLP_TILE, D_MODEL), BF16)],
        compiler_params=pltpu.CompilerParams(
            dimension_semantics=("arbitrary",),
            vmem_limit_bytes=VMEM_LIMIT_BYTES),
        name="mlp",
    )(h2d, ng, wup, wdn, fg)


def _row(v):
    return v.reshape(1, -1).astype(F32)


def _row8(v):
    return jnp.broadcast_to(v.reshape(1, -1).astype(F32), (SUBLANES, v.size))


def _pad_lanes(v):
    return jnp.pad(v.astype(F32), ((0, 0), (0, LANES - v.shape[-1])))


def kernel(x, norm_mix_g, w_in, b_gate, gmlp_ln_g, gmlp_ln_b, gmlp_w_s, gmlp_b_s, lru_conv_w, lru_conv_b, lru_w_r, lru_b_r, lru_w_i, lru_b_i, lru_lambda, ssd_conv_w, ssd_conv_b, ssd_dt_bias, ssd_a_log, ssd_d, ssd_norm_g, w_branch_a, w_branch_b, w_branch_c, w_out, norm_mlp_g, w_mlp_up, w_mlp_down, final_norm_g):
    bsz, seq, d = x.shape
    assert d == D_MODEL and seq % TQ == 0 and (bsz * seq) % MLP_TILE == 0
    depth = w_in.shape[0]
    o_dt = COL_DT
    o_g = o_dt + SSD_HEADS
    assert w_in.shape[-1] == o_g + 3 * D_MODEL
    w_main_b = w_in[:, :, :o_dt].astype(BF16)
    w_dt_b = jnp.pad(w_in[:, :, o_dt:o_g], ((0, 0), (0, 0), (0, LANES - SSD_HEADS))).astype(BF16)
    w_g_b = w_in[:, :, o_g:].astype(BF16)
    w_ri_b = jnp.concatenate([lru_w_r, lru_w_i], axis=-1).astype(BF16)
    w_ba_b, w_bb_b, w_bc_b, w_out_b = (w.astype(BF16) for w in (w_branch_a, w_branch_b, w_branch_c, w_out))
    w_up_b, w_dn_b = w_mlp_up.astype(BF16), w_mlp_down.astype(BF16)
    h = x
    for l in range(depth):
        params = (
            (_row8(norm_mix_g[l]), None),
            (w_main_b, l),
            (w_dt_b, l),
            (w_g_b, l),
            (_row8(b_gate[l]), None),
            (_row8(gmlp_ln_g[l]), None),
            (_row8(gmlp_ln_b[l]), None),
            (gmlp_w_s, l),
            (jnp.repeat(gmlp_b_s[l].T, LANES, axis=1), None),
            (lru_conv_w, l),
            (_row(lru_conv_b[l]), None),
            (w_ri_b, l),
            (_row8(lru_b_r[l]), None),
            (_row8(lru_b_i[l]), None),
            (_row8(lru_lambda[l]), None),
            (ssd_conv_w, l),
            (_row(ssd_conv_b[l]), None),
            (_pad_lanes(_row(ssd_dt_bias[l])), None),
            (_pad_lanes(_row(ssd_a_log[l])), None),
            (_row8(jnp.repeat(ssd_d[l], SSD_HEAD_DIM)), None),
            (_row8(ssd_norm_g[l]), None),
            (w_ba_b, l),
            (w_bb_b, l),
            (w_bc_b, l),
            (w_out_b, l),
        )
        h = _mixer_call(h, params)
        h2d = _mlp_call(h.reshape(bsz * seq, d), _row8(norm_mlp_g[l]), w_up_b, w_dn_b, l,
                        _row(final_norm_g), final_norm=(l == depth - 1))
        h = h2d.reshape(bsz, seq, d)
    return h
```

```python
import functools

import jax
import jax.numpy as jnp
from jax import lax
from jax.experimental import pallas as pl
from jax.experimental.pallas import tpu as pltpu

F32 = jnp.float32
BF16 = jnp.bfloat16

LANES = 128
SUBLANES = 8
VMEM_LIMIT_BYTES = 60 * 1024 * 1024

EPS = 1e-6
D_MODEL = 1024
CONV_WIDTH = 4
CHUNK = 128
TQ = 256
NCH = TQ // CHUNK
GMLP_GROUPS = 8
LRU_HEADS = 8
LRU_C = 8.0
SSD_HEADS = 16
SSD_HEAD_DIM = 64
SSD_GROUPS = 4
SSD_HPG = SSD_HEADS // SSD_GROUPS
SSD_STATE = 128
SSD_GW = SSD_HPG * SSD_HEAD_DIM
MLP_HIDDEN = 4 * D_MODEL
CONV_COLS = 256
DOT_COLS = 512
COL_U = 0
COL_V = COL_U + D_MODEL
COL_XB = COL_V + D_MODEL
COL_GATE = COL_XB + D_MODEL
COL_Z = COL_GATE + D_MODEL
COL_XBC = COL_Z + D_MODEL
COL_DT = COL_XBC + 2 * D_MODEL
MLP_TILE = 1024
MLP_HCHUNK = 1024


def _dot(a, b):
    return jnp.dot(a, b, preferred_element_type=F32)


def _rows(i, n):
    return slice(i * n, (i + 1) * n)


class _Plan:
    def __init__(self):
        self.steps = []

    def add(self, fn, mxu=0, valu=0, deps=()):
        self.steps.append((fn, mxu, valu, tuple(deps)))
        return len(self.steps) - 1

    def run(self):
        n = len(self.steps)
        users = [[] for _ in range(n)]
        for i, (_, _, _, deps) in enumerate(self.steps):
            for d in deps:
                users[d].append(i)
        tail = [0.0] * n
        for i in reversed(range(n)):
            _, m, v, _ = self.steps[i]
            tail[i] = m + v + max((tail[u] for u in users[i]), default=0.0)
        finish = [None] * n
        clock = {"m": 0.0, "v": 0.0}
        pending = set(range(n))
        while pending:
            best = None
            for i in pending:
                _, m, v, deps = self.steps[i]
                if any(finish[d] is None for d in deps):
                    continue
                start = max([finish[d] for d in deps] + [clock["m"] if m else 0.0, clock["v"] if v else 0.0])
                key = (start, -tail[i], i)
                if best is None or key < best[0]:
                    best = (key, i, start)
            _, i, start = best
            fn, m, v, _ = self.steps[i]
            fn()
            if m:
                clock["m"] = start + m
            if v:
                clock["v"] = start + v
            finish[i] = start + max(m, v)
            pending.remove(i)


def _unrolled(n, body, carry):
    for i in range(n):
        carry = body(i, carry)
    return carry


def _shift_rows(x, d, fill):
    row = lax.broadcasted_iota(jnp.int32, x.shape, 0)
    return jnp.where(row >= d, pltpu.roll(x, d, 0), fill)


def _causal_conv_cols(buf_ref, cols, w_ref, b_ref):
    blk = buf_ref[:, cols]
    acc = blk[SUBLANES:] * w_ref[CONV_WIDTH - 1:CONV_WIDTH, cols] + b_ref[:, cols]
    for k in range(1, CONV_WIDTH):
        tap = pltpu.roll(blk, k, 0)[SUBLANES:]
        acc = acc + tap * w_ref[CONV_WIDTH - 1 - k:CONV_WIDTH - k, cols]
    return acc


def _mixer_kernel(x_ref, ng_ref, win_ref, wdt_ref, wg_ref,
                  bgate_ref, lng_ref, lnb_ref, ws_ref, bsfull_ref,
                  lcw_ref, lcb_ref, wri_ref, br_ref, bi_ref, lam_ref,
                  scw_ref, scb_ref, dtb_ref, alog_ref, dskip_ref, sng_ref,
                  wba_ref, wbb_ref, wbc_ref, wo_ref,
                  o_ref,
                  hn_s, za_s, v_s, ya_s, xb_s, gate_s, xc_s, ri_s, yb_s,
                  z_s, xbc_s, xbcc_s, dt_s, y_s, yc_s, graw_s, p_s, mg_s,
                  lruh_s, state_s):
    n_blk8 = TQ // SUBLANES
    n_blk16 = TQ // (2 * SUBLANES)

    @pl.when(pl.program_id(1) == 0)
    def _reset_state():
        lruh_s[...] = jnp.zeros_like(lruh_s)
        state_s[...] = jnp.zeros_like(state_s)
        xb_s[0:SUBLANES, :] = jnp.zeros((SUBLANES, xb_s.shape[1]), F32)
        xbc_s[0:SUBLANES, :] = jnp.zeros((SUBLANES, xbc_s.shape[1]), F32)


    def dots(dst_ref, row0, lhs_ref, w_ref, n_cols, w_col0=0, dst_col0=0):
        out = []
        for c0 in range(0, n_cols, DOT_COLS):
            n = min(DOT_COLS, n_cols - c0)

            def piece(c0=c0, n=n):
                dst_ref[row0:row0 + TQ, dst_col0 + c0:dst_col0 + c0 + n] = _dot(
                    lhs_ref[...], w_ref[:, w_col0 + c0:w_col0 + c0 + n])
            out.append((n, piece))
        return out

    def norm_step(i):
        parts = []
        for half in range(2):
            x = x_ref[0, _rows(2 * i + half, SUBLANES), :]
            ms = jnp.mean(x * x, axis=-1, keepdims=True)
            parts.append(x * lax.rsqrt(ms + EPS) * ng_ref[...])
        hn_s[_rows(i, 2 * SUBLANES), :] = jnp.concatenate(parts, axis=0).astype(BF16)

    def gmlp_v_step(i):
        rows = _rows(i, SUBLANES)
        v = jax.nn.gelu(za_s[rows, D_MODEL:2 * D_MODEL])
        mu = jnp.mean(v, axis=-1, keepdims=True)
        vc = v - mu
        var = jnp.mean(vc * vc, axis=-1, keepdims=True)
        v_s[rows, :] = vc * lax.rsqrt(var + EPS) * lng_ref[...] + lnb_ref[...]

    def gmlp_u_step(i):
        rows = _rows(i, SUBLANES)
        za_s[rows, 0:D_MODEL] = jax.nn.gelu(za_s[rows, 0:D_MODEL])

    def gmlp_mix_step(c, g):
        tril = (lax.broadcasted_iota(jnp.int32, (CHUNK, CHUNK), 0)
                >= lax.broadcasted_iota(jnp.int32, (CHUNK, CHUNK), 1))
        rws = _rows(c, CHUNK)
        cols = slice(g * LANES, (g + 1) * LANES)
        w = jnp.where(tril, ws_ref[g], 0.0).astype(BF16)
        mixed = _dot(w, v_s[rws, cols].astype(BF16)) + bsfull_ref[:, cols]
        ya_s[rws, cols] = (za_s[rws, cols] * mixed).astype(BF16)

    def lru_conv_step(q):
        cols = slice(q * CONV_COLS, (q + 1) * CONV_COLS)
        xc_s[:, cols] = _causal_conv_cols(xb_s, cols, lcw_ref, lcb_ref)

    def lru_history_step():
        xb_s[0:SUBLANES, :] = xb_s[TQ:TQ + SUBLANES, :]

    def ri_step(h):
        cols = slice(h * LANES, (h + 1) * LANES)
        ri_s[:, 2 * h * LANES:2 * (h + 1) * LANES] = _dot(xc_s[:, cols].astype(BF16), wri_ref[h])

    lru_carry = [None]

    def lru_scan_block(i, h_prev):
        rows = _rows(i, SUBLANES)
        ri = ri_s[rows, :]
        r_pre = jnp.concatenate(
            [ri[:, 2 * h * LANES:(2 * h + 1) * LANES] for h in range(LRU_HEADS)], axis=1)
        i_pre = jnp.concatenate(
            [ri[:, (2 * h + 1) * LANES:(2 * h + 2) * LANES] for h in range(LRU_HEADS)], axis=1)
        r = jax.nn.sigmoid(r_pre + br_ref[...])
        ig = jax.nn.sigmoid(i_pre + bi_ref[...])
        xc = xc_s[rows, :]
        log_a = r * (-LRU_C * jax.nn.softplus(-lam_ref[...]))
        a_cum = jnp.exp(log_a)
        b_cum = jnp.sqrt(-jnp.tanh(log_a) * (a_cum * a_cum + 1.0)) * (ig * xc)
        for d in (1, 2, 4):
            b_cum = a_cum * _shift_rows(b_cum, d, 0.0) + b_cum
            a_cum = a_cum * _shift_rows(a_cum, d, 1.0)
        hseq = a_cum * h_prev + b_cum
        y = jax.nn.gelu(gate_s[rows, :]) * hseq
        return y, jnp.broadcast_to(hseq[SUBLANES - 1:SUBLANES, :], hseq.shape)

    def lru_scan_step(i):
        if lru_carry[0] is None:
            lru_carry[0] = lruh_s[...]
        y0, h_mid = lru_scan_block(2 * i, lru_carry[0])
        y1, lru_carry[0] = lru_scan_block(2 * i + 1, h_mid)
        yb_s[_rows(i, 2 * SUBLANES), :] = jnp.concatenate([y0, y1], axis=0).astype(BF16)
        if i == n_blk16 - 1:
            lruh_s[...] = lru_carry[0]

    def ssd_conv_step(q):
        cols = slice(q * CONV_COLS, (q + 1) * CONV_COLS)
        xbcc_s[:, cols] = jax.nn.silu(_causal_conv_cols(xbc_s, cols, scw_ref, scb_ref))

    def ssd_history_step():
        xbc_s[0:SUBLANES, :] = xbc_s[TQ:TQ + SUBLANES, :]

    pres = [dict() for _ in range(NCH)]

    def ssd_prelude_step(c):
        pre = pres[c]
        dt = jax.nn.softplus(dt_s[_rows(c, CHUNK), :] + dtb_ref[...])
        cs = dt * (-jnp.exp(alog_ref[...]))
        for d in (1, 2, 4, 8, 16, 32, 64):
            cs = cs + _shift_rows(cs, d, 0.0)
        cs_t = cs.T
        dt_t = dt.T
        pre.update(
            cs=cs, cs_t=cs_t, dt_t=dt_t,
            e_cs=jnp.exp(cs),
            w_t=dt_t * jnp.exp(cs_t[:, CHUNK - 1:CHUNK] - cs_t))

    def ssd_group_step(c, g):
        pre = pres[c]
        rws = _rows(c, CHUNK)
        cs, cs_t, dt_t, e_cs, w_t = pre["cs"], pre["cs_t"], pre["dt_t"], pre["e_cs"], pre["w_t"]
        tril = (lax.broadcasted_iota(jnp.int32, (CHUNK, CHUNK), 0)
                >= lax.broadcasted_iota(jnp.int32, (CHUNK, CHUNK), 1))
        lane_gw = lax.broadcasted_iota(jnp.int32, (CHUNK, SSD_GW), 1)
        lane_row = lax.broadcasted_iota(jnp.int32, (1, SSD_GW), 1)
        xcols = slice(g * SSD_GW, (g + 1) * SSD_GW)
        bcols = slice(D_MODEL + g * SSD_STATE, D_MODEL + (g + 1) * SSD_STATE)
        ccols = slice(D_MODEL + SSD_GROUPS * SSD_STATE + g * SSD_STATE,
                      D_MODEL + SSD_GROUPS * SSD_STATE + (g + 1) * SSD_STATE)
        xg = xbcc_s[rws, xcols]
        bg = xbcc_s[rws, bcols]
        cg = xbcc_s[rws, ccols]
        state = state_s[g]
        cb = lax.dot_general(cg.astype(BF16), bg.astype(BF16), (((1,), (1,)), ((), ())),
                             preferred_element_type=F32)
        bg_t = bg.T
        m_parts, ce_parts, bw_parts, x_parts, st_parts = [], [], [], [], []
        dec_row = jnp.zeros((1, SSD_GW), F32)
        for r in range(SSD_HPG):
            h = g * SSD_HPG + r
            seg = jnp.where(tril, cs[:, h:h + 1] - cs_t[h:h + 1, :], -jnp.inf)
            m_parts.append((cb * jnp.exp(seg) * dt_t[h:h + 1, :]).astype(BF16))
            ce_parts.append((cg * e_cs[:, h:h + 1]).astype(BF16))
            bw_parts.append((bg_t * w_t[h:h + 1, :]).astype(BF16))
            in_head = (lane_gw >= r * SSD_HEAD_DIM) & (lane_gw < (r + 1) * SSD_HEAD_DIM)
            x_parts.append(jnp.where(in_head, xg, 0.0).astype(BF16))
            st_parts.append(jnp.where(in_head, state, 0.0).astype(BF16))
            in_head_row = (lane_row >= r * SSD_HEAD_DIM) & (lane_row < (r + 1) * SSD_HEAD_DIM)
            dec_row = jnp.where(in_head_row, e_cs[CHUNK - 1:CHUNK, h:h + 1], dec_row)
        x_blk = jnp.concatenate(x_parts, axis=0)
        lhs = jnp.concatenate(m_parts + ce_parts, axis=1)
        rhs = jnp.concatenate([x_blk] + st_parts, axis=0)
        y_s[rws, xcols] = _dot(lhs, rhs)
        state_s[g] = state * dec_row + _dot(jnp.concatenate(bw_parts, axis=1), x_blk)

    def ssd_out_step(i):
        outs = []
        for half in range(2):
            rows = _rows(2 * i + half, SUBLANES)
            y = y_s[rows, :] + xbcc_s[rows, 0:D_MODEL] * dskip_ref[...]
            y = y * jax.nn.silu(z_s[rows, :])
            parts = []
            for g in range(SSD_GROUPS):
                yg = y[:, g * SSD_GW:(g + 1) * SSD_GW]
                parts.append(yg * lax.rsqrt(jnp.mean(yg * yg, axis=-1, keepdims=True) + EPS))
            outs.append(jnp.concatenate(parts, axis=1) * sng_ref[...])
        yc_s[_rows(i, 2 * SUBLANES), :] = jnp.concatenate(outs, axis=0).astype(BF16)

    def gated(rows, k):
        cols = slice(k * D_MODEL, (k + 1) * D_MODEL)
        return jax.nn.sigmoid(graw_s[rows, cols] + bgate_ref[:, cols]) * p_s[rows, cols]

    def merge01_step(i):
        rows = _rows(i, SUBLANES)
        p_s[rows, 0:D_MODEL] = gated(rows, 0) + gated(rows, 1)

    def merge2_step(i):
        halves = [p_s[_rows(2 * i + half, SUBLANES), 0:D_MODEL] + gated(_rows(2 * i + half, SUBLANES), 2)
                  for half in range(2)]
        mg_s[_rows(i, 2 * SUBLANES), :] = jnp.concatenate(halves, axis=0).astype(BF16)

    def out_step(q):
        cols = slice(q * DOT_COLS, (q + 1) * DOT_COLS)
        o_ref[0, :, cols] = x_ref[0, :, cols] + _dot(mg_s[...], wo_ref[:, cols])

    plan = _Plan()

    def add_each(fn, n, deps, mxu=0, valu=0):
        return [plan.add(functools.partial(fn, i), mxu, valu, deps) for i in range(n)]

    def add_dots(dst_ref, row0, lhs_ref, w_ref, n_cols, deps, w_col0=0, dst_col0=0):
        return [plan.add(fn, mxu=NCH * n / 2, deps=deps)
                for n, fn in dots(dst_ref, row0, lhs_ref, w_ref, n_cols, w_col0, dst_col0)]

    def piece_of(ids, col):
        return [ids[col // DOT_COLS]]

    norm = add_each(norm_step, n_blk16, (), valu=30)
    d_v = add_dots(za_s, 0, hn_s, win_ref, D_MODEL, norm, w_col0=COL_V, dst_col0=D_MODEL)
    d_u = add_dots(za_s, 0, hn_s, win_ref, D_MODEL, norm, w_col0=COL_U)
    d_xb = add_dots(xb_s, SUBLANES, hn_s, win_ref, D_MODEL, norm, w_col0=COL_XB)
    d_xbc = add_dots(xbc_s, SUBLANES, hn_s, win_ref, 2 * D_MODEL, norm, w_col0=COL_XBC)
    d_dt = add_dots(dt_s, 0, hn_s, wdt_ref, LANES, norm)
    d_gate = add_dots(gate_s, 0, hn_s, win_ref, D_MODEL, norm, w_col0=COL_GATE)
    d_z = add_dots(z_s, 0, hn_s, win_ref, D_MODEL, norm, w_col0=COL_Z)
    d_g = add_dots(graw_s, 0, hn_s, wg_ref, 3 * D_MODEL, norm)

    gv = add_each(gmlp_v_step, n_blk8, d_v, valu=70)
    gu = add_each(gmlp_u_step, n_blk8, d_u, valu=30)
    mix = [plan.add(functools.partial(gmlp_mix_step, c, g), 32, 40, gv + gu)
           for c in range(NCH) for g in range(GMLP_GROUPS)]
    p_a = add_dots(p_s, 0, ya_s, wba_ref, D_MODEL, mix)

    lconv = [plan.add(functools.partial(lru_conv_step, q), valu=110 * NCH, deps=piece_of(d_xb, q * CONV_COLS))
             for q in range(D_MODEL // CONV_COLS)]
    plan.add(lru_history_step, valu=2, deps=lconv)
    ri = [plan.add(functools.partial(ri_step, h), mxu=32 * NCH, valu=4 * NCH, deps=[lconv[h * LANES // CONV_COLS]])
          for h in range(LRU_HEADS)]
    scan = []
    for i in range(n_blk16):
        scan.append(plan.add(functools.partial(lru_scan_step, i), valu=220, deps=ri + d_gate + scan[-1:]))
    p_b = add_dots(p_s, 0, yb_s, wbb_ref, D_MODEL, scan, dst_col0=D_MODEL)

    sconv = [plan.add(functools.partial(ssd_conv_step, q), valu=160 * NCH, deps=piece_of(d_xbc, q * CONV_COLS))
             for q in range(2 * D_MODEL // CONV_COLS)]
    plan.add(ssd_history_step, valu=4, deps=sconv)
    grp = []
    for c in range(NCH):
        prelude = plan.add(functools.partial(ssd_prelude_step, c), valu=100, deps=d_dt)
        grp += [plan.add(functools.partial(ssd_group_step, c, g), 220, 250,
                         sconv + [prelude] + grp[(c - 1) * SSD_GROUPS + g:(c - 1) * SSD_GROUPS + g + 1] * (c > 0))
                for g in range(SSD_GROUPS)]
    sout = add_each(ssd_out_step, n_blk16, grp + d_z, valu=80)
    p_c = add_dots(p_s, 0, yc_s, wbc_ref, D_MODEL, sout, dst_col0=2 * D_MODEL)

    m01 = add_each(merge01_step, n_blk8, p_a + p_b + d_g, valu=27)
    m2 = add_each(merge2_step, n_blk16, m01 + p_c, valu=25)
    add_each(out_step, D_MODEL // DOT_COLS, m2, mxu=NCH * DOT_COLS / 2, valu=16 * NCH)
    plan.run()


def _mlp_kernel(x_ref, ng_ref, wup_ref, wdn_ref, fg_ref, o_ref, hn_s, *, final_norm):
    n_blk16 = MLP_TILE // (2 * SUBLANES)

    def norm_body(i, c):
        parts = []
        for half in range(2):
            x = x_ref[_rows(2 * i + half, SUBLANES), :]
            ms = jnp.mean(x * x, axis=-1, keepdims=True)
            parts.append(x * lax.rsqrt(ms + EPS) * ng_ref[...])
        hn_s[_rows(i, 2 * SUBLANES), :] = jnp.concatenate(parts, axis=0).astype(BF16)
        return c
    _unrolled(n_blk16, norm_body, 0)
    hn = hn_s[...]
    acc = x_ref[...]
    for c in range(MLP_HIDDEN // MLP_HCHUNK):
        cols = slice(c * MLP_HCHUNK, (c + 1) * MLP_HCHUNK)
        up = jnp.maximum(_dot(hn, wup_ref[:, cols]), 0.0)
        acc = acc + _dot((up * up).astype(BF16), wdn_ref[cols, :])
    if final_norm:
        ms = jnp.mean(acc * acc, axis=-1, keepdims=True)
        acc = acc * lax.rsqrt(ms + EPS) * fg_ref[...]
    o_ref[...] = acc


def _const_spec(shape):
    return pl.BlockSpec(shape, lambda *_: (0,) * len(shape), pipeline_mode=pl.Buffered(1))


def _layer_spec(shape, layer):
    rest = tuple(shape[1:])
    return pl.BlockSpec((None,) + rest, lambda *_: (layer,) + (0,) * len(rest),
                        pipeline_mode=pl.Buffered(1))


def _param_spec(p, layer):
    return _const_spec(p.shape) if layer is None else _layer_spec(p.shape, layer)


def _mixer_call(h, params):
    bsz, seq, _ = h.shape
    x_spec = pl.BlockSpec((1, TQ, D_MODEL), lambda b, t: (b, t, 0))
    in_specs = [x_spec] + [_param_spec(p, layer) for p, layer in params]
    scratch = [
        pltpu.VMEM((TQ, D_MODEL), BF16),
        pltpu.VMEM((TQ, 2 * D_MODEL), F32),
        pltpu.VMEM((TQ, D_MODEL), F32),
        pltpu.VMEM((TQ, D_MODEL), BF16),
        pltpu.VMEM((TQ + SUBLANES, D_MODEL), F32),
        pltpu.VMEM((TQ, D_MODEL), F32),
        pltpu.VMEM((TQ, D_MODEL), F32),
        pltpu.VMEM((TQ, 2 * D_MODEL), F32),
        pltpu.VMEM((TQ, D_MODEL), BF16),
        pltpu.VMEM((TQ, D_MODEL), F32),
        pltpu.VMEM((TQ + SUBLANES, 2 * D_MODEL), F32),
        pltpu.VMEM((TQ, 2 * D_MODEL), F32),
        pltpu.VMEM((TQ, LANES), F32),
        pltpu.VMEM((TQ, D_MODEL), F32),
        pltpu.VMEM((TQ, D_MODEL), BF16),
        pltpu.VMEM((TQ, 3 * D_MODEL), F32),
        pltpu.VMEM((TQ, 3 * D_MODEL), F32),
        pltpu.VMEM((TQ, D_MODEL), BF16),
        pltpu.VMEM((SUBLANES, D_MODEL), F32),
        pltpu.VMEM((SSD_GROUPS, SSD_STATE, SSD_GW), F32),
    ]
    return pl.pallas_call(
        _mixer_kernel,
        out_shape=jax.ShapeDtypeStruct(h.shape, F32),
        grid=(bsz, seq // TQ),
        in_specs=in_specs,
        out_specs=x_spec,
        scratch_shapes=scratch,
        compiler_params=pltpu.CompilerParams(
            dimension_semantics=("arbitrary", "arbitrary"),
            vmem_limit_bytes=VMEM_LIMIT_BYTES),
        name="mixer",
    )(h, *[p for p, _ in params])


def _mlp_call(h2d, ng, wup, wdn, layer, fg, final_norm):
    n_tok = h2d.shape[0]
    x_spec = pl.BlockSpec((MLP_TILE, D_MODEL), lambda i: (i, 0))
    return pl.pallas_call(
        functools.partial(_mlp_kernel, final_norm=final_norm),
        out_shape=jax.ShapeDtypeStruct(h2d.shape, F32),
        grid=(n_tok // MLP_TILE,),
        in_specs=[x_spec, _const_spec(ng.shape), _layer_spec(wup.shape, layer),
                  _layer_spec(wdn.shape, layer), _const_spec(fg.shape)],
        out_specs=x_spec,
        scratch_shapes=[pltpu.VMEM((MLP_TILE, D_MODEL), BF16)],
        compiler_params=pltpu.CompilerParams(
            dimension_semantics=("arbitrary",),
            vmem_limit_bytes=VMEM_LIMIT_BYTES),
        name="mlp",
    )(h2d, ng, wup, wdn, fg)


def _row(v):
    return v.reshape(1, -1).astype(F32)


def _row8(v):
    return jnp.broadcast_to(v.reshape(1, -1).astype(F32), (SUBLANES, v.size))


def _pad_lanes(v):
    return jnp.pad(v.astype(F32), ((0, 0), (0, LANES - v.shape[-1])))


def kernel(x, norm_mix_g, w_in, b_gate, gmlp_ln_g, gmlp_ln_b, gmlp_w_s, gmlp_b_s, lru_conv_w, lru_conv_b, lru_w_r, lru_b_r, lru_w_i, lru_b_i, lru_lambda, ssd_conv_w, ssd_conv_b, ssd_dt_bias, ssd_a_log, ssd_d, ssd_norm_g, w_branch_a, w_branch_b, w_branch_c, w_out, norm_mlp_g, w_mlp_up, w_mlp_down, final_norm_g):
    bsz, seq, d = x.shape
    assert d == D_MODEL and seq % TQ == 0 and (bsz * seq) % MLP_TILE == 0
    depth = w_in.shape[0]
    o_dt = COL_DT
    o_g = o_dt + SSD_HEADS
    assert w_in.shape[-1] == o_g + 3 * D_MODEL
    w_main_b = w_in[:, :, :o_dt].astype(BF16)
    w_dt_b = jnp.pad(w_in[:, :, o_dt:o_g], ((0, 0), (0, 0), (0, LANES - SSD_HEADS))).astype(BF16)
    w_g_b = w_in[:, :, o_g:].astype(BF16)
    w_ri_b = jnp.concatenate([lru_w_r, lru_w_i], axis=-1).astype(BF16)
    w_ba_b, w_bb_b, w_bc_b, w_out_b = (w.astype(BF16) for w in (w_branch_a, w_branch_b, w_branch_c, w_out))
    w_up_b, w_dn_b = w_mlp_up.astype(BF16), w_mlp_down.astype(BF16)
    h = x
    for l in range(depth):
        params = (
            (_row8(norm_mix_g[l]), None),
            (w_main_b, l),
            (w_dt_b, l),
            (w_g_b, l),
            (_row8(b_gate[l]), None),
            (_row8(gmlp_ln_g[l]), None),
            (_row8(gmlp_ln_b[l]), None),
            (gmlp_w_s, l),
            (jnp.repeat(gmlp_b_s[l].T, LANES, axis=1), None),
            (lru_conv_w, l),
            (_row(lru_conv_b[l]), None),
            (w_ri_b, l),
            (_row8(lru_b_r[l]), None),
            (_row8(lru_b_i[l]), None),
            (_row8(lru_lambda[l]), None),
            (ssd_conv_w, l),
            (_row(ssd_conv_b[l]), None),
            (_pad_lanes(_row(ssd_dt_bias[l])), None),
            (_pad_lanes(_row(ssd_a_log[l])), None),
            (_row8(jnp.repeat(ssd_d[l], SSD_HEAD_DIM)), None),
            (_row8(ssd_norm_g[l]), None),
            (w_ba_b, l),
            (w_bb_b, l),
            (w_bc_b, l),
            (w_out_b, l),
        )
        h = _mixer_call(h, params)
        h2d = _mlp_call(h.reshape(bsz * seq, d), _row8(norm_mlp_g[l]), w_up_b, w_dn_b, l,
                        _row(final_norm_g), final_norm=(l == depth - 1))
        h = h2d.reshape(bsz, seq, d)
    return h
```

```python
import functools

import jax
import jax.numpy as jnp
from jax import lax
from jax.experimental import pallas as pl
from jax.experimental.pallas import tpu as pltpu

F32 = jnp.float32
BF16 = jnp.bfloat16

LANES = 128
SUBLANES = 8
VMEM_LIMIT_BYTES = 60 * 1024 * 1024

EPS = 1e-6
D_MODEL = 1024
CONV_WIDTH = 4
CHUNK = 128
TQ = 256
NCH = TQ // CHUNK
GMLP_GROUPS = 8
LRU_HEADS = 8
LRU_C = 8.0
SSD_HEADS = 16
SSD_HEAD_DIM = 64
SSD_GROUPS = 4
SSD_HPG = SSD_HEADS // SSD_GROUPS
SSD_STATE = 128
SSD_GW = SSD_HPG * SSD_HEAD_DIM
MLP_HIDDEN = 4 * D_MODEL
CONV_COLS = 256
DOT_COLS = 512
COL_U = 0
COL_V = COL_U + D_MODEL
COL_XB = COL_V + D_MODEL
COL_GATE = COL_XB + D_MODEL
COL_Z = COL_GATE + D_MODEL
COL_XBC = COL_Z + D_MODEL
COL_DT = COL_XBC + 2 * D_MODEL
MLP_TILE = 1024
MLP_HCHUNK = 1024
TCAST_ROWS = 512


def _dot(a, b):
    return jnp.dot(a, b, preferred_element_type=F32)


def _rows(i, n):
    return slice(i * n, (i + 1) * n)


class _Plan:
    def __init__(self):
        self.steps = []

    def add(self, fn, mxu=0, valu=0, deps=()):
        self.steps.append((fn, mxu, valu, tuple(deps)))
        return len(self.steps) - 1

    def run(self):
        n = len(self.steps)
        users = [[] for _ in range(n)]
        for i, (_, _, _, deps) in enumerate(self.steps):
            for d in deps:
                users[d].append(i)
        tail = [0.0] * n
        for i in reversed(range(n)):
            _, m, v, _ = self.steps[i]
            tail[i] = m + v + max((tail[u] for u in users[i]), default=0.0)
        finish = [None] * n
        clock = {"m": 0.0, "v": 0.0}
        pending = set(range(n))
        while pending:
            best = None
            for i in pending:
                _, m, v, deps = self.steps[i]
                if any(finish[d] is None for d in deps):
                    continue
                start = max([finish[d] for d in deps] + [clock["m"] if m else 0.0, clock["v"] if v else 0.0])
                key = (start, -tail[i], i)
                if best is None or key < best[0]:
                    best = (key, i, start)
            _, i, start = best
            fn, m, v, _ = self.steps[i]
            fn()
            if m:
                clock["m"] = start + m
            if v:
                clock["v"] = start + v
            finish[i] = start + max(m, v)
            pending.remove(i)


def _unrolled(n, body, carry):
    for i in range(n):
        carry = body(i, carry)
    return carry


def _shift_rows(x, d, fill):
    row = lax.broadcasted_iota(jnp.int32, x.shape, 0)
    return jnp.where(row >= d, pltpu.roll(x, d, 0), fill)


def _causal_conv_cols(buf_ref, cols, w_ref, b_ref):
    blk = buf_ref[:, cols]
    acc = blk[SUBLANES:] * w_ref[CONV_WIDTH - 1:CONV_WIDTH, cols] + b_ref[:, cols]
    for k in range(1, CONV_WIDTH):
        tap = pltpu.roll(blk, k, 0)[SUBLANES:]
        acc = acc + tap * w_ref[CONV_WIDTH - 1 - k:CONV_WIDTH - k, cols]
    return acc


def _mixer_kernel(x_ref, ng_ref, win_ref, wdt_ref, wg_ref,
                  bgate_ref, lng_ref, lnb_ref, ws_ref, bsfull_ref,
                  lcw_ref, lcb_ref, wri_ref, br_ref, bi_ref, lam_ref,
                  scw_ref, scb_ref, dtb_ref, alog_ref, dskip_ref, sng_ref,
                  wba_ref, wbb_ref, wbc_ref, wo_ref,
                  o_ref,
                  hn_s, za_s, v_s, ya_s, xb_s, gate_s, xc_s, ri_s, yb_s,
                  z_s, xbc_s, xbcc_s, dt_s, y_s, yc_s, graw_s, p_s, mg_s,
                  lruh_s, state_s):
    n_blk8 = TQ // SUBLANES
    n_blk16 = TQ // (2 * SUBLANES)

    @pl.when(pl.program_id(1) == 0)
    def _reset_state():
        lruh_s[...] = jnp.zeros_like(lruh_s)
        state_s[...] = jnp.zeros_like(state_s)
        xb_s[0:SUBLANES, :] = jnp.zeros((SUBLANES, xb_s.shape[1]), F32)
        xbc_s[0:SUBLANES, :] = jnp.zeros((SUBLANES, xbc_s.shape[1]), F32)


    def dots(dst_ref, row0, lhs_ref, w_ref, n_cols, w_col0=0, dst_col0=0):
        out = []
        for c0 in range(0, n_cols, DOT_COLS):
            n = min(DOT_COLS, n_cols - c0)

            def piece(c0=c0, n=n):
                dst_ref[row0:row0 + TQ, dst_col0 + c0:dst_col0 + c0 + n] = _dot(
                    lhs_ref[...], w_ref[:, w_col0 + c0:w_col0 + c0 + n])
            out.append((n, piece))
        return out

    def norm_step(i):
        parts = []
        for half in range(2):
            x = x_ref[0, _rows(2 * i + half, SUBLANES), :]
            ms = jnp.mean(x * x, axis=-1, keepdims=True)
            parts.append(x * lax.rsqrt(ms + EPS) * ng_ref[...])
        hn_s[_rows(i, 2 * SUBLANES), :] = jnp.concatenate(parts, axis=0).astype(BF16)

    def gmlp_v_step(i):
        rows = _rows(i, SUBLANES)
        v = jax.nn.gelu(za_s[rows, D_MODEL:2 * D_MODEL])
        mu = jnp.mean(v, axis=-1, keepdims=True)
        vc = v - mu
        var = jnp.mean(vc * vc, axis=-1, keepdims=True)
        v_s[rows, :] = vc * lax.rsqrt(var + EPS) * lng_ref[...] + lnb_ref[...]

    def gmlp_u_step(i):
        rows = _rows(i, SUBLANES)
        za_s[rows, 0:D_MODEL] = jax.nn.gelu(za_s[rows, 0:D_MODEL])

    def gmlp_mix_step(c, g):
        tril = (lax.broadcasted_iota(jnp.int32, (CHUNK, CHUNK), 0)
                >= lax.broadcasted_iota(jnp.int32, (CHUNK, CHUNK), 1))
        rws = _rows(c, CHUNK)
        cols = slice(g * LANES, (g + 1) * LANES)
        w = jnp.where(tril, ws_ref[g], 0.0).astype(BF16)
        mixed = _dot(w, v_s[rws, cols].astype(BF16)) + bsfull_ref[:, cols]
        ya_s[rws, cols] = (za_s[rws, cols] * mixed).astype(BF16)

    def lru_conv_step(q):
        cols = slice(q * CONV_COLS, (q + 1) * CONV_COLS)
        xc_s[:, cols] = _causal_conv_cols(xb_s, cols, lcw_ref, lcb_ref)

    def lru_history_step():
        xb_s[0:SUBLANES, :] = xb_s[TQ:TQ + SUBLANES, :]

    def ri_step(h):
        cols = slice(h * LANES, (h + 1) * LANES)
        ri_s[:, 2 * h * LANES:2 * (h + 1) * LANES] = _dot(xc_s[:, cols].astype(BF16), wri_ref[h])

    lru_carry = [None]

    def lru_scan_block(i, h_prev):
        rows = _rows(i, SUBLANES)
        ri = ri_s[rows, :]
        r_pre = jnp.concatenate(
            [ri[:, 2 * h * LANES:(2 * h + 1) * LANES] for h in range(LRU_HEADS)], axis=1)
        i_pre = jnp.concatenate(
            [ri[:, (2 * h + 1) * LANES:(2 * h + 2) * LANES] for h in range(LRU_HEADS)], axis=1)
        r = jax.nn.sigmoid(r_pre + br_ref[...])
        ig = jax.nn.sigmoid(i_pre + bi_ref[...])
        xc = xc_s[rows, :]
        log_a = r * (-LRU_C * jax.nn.softplus(-lam_ref[...]))
        a_cum = jnp.exp(log_a)
        b_cum = jnp.sqrt(-jnp.tanh(log_a) * (a_cum * a_cum + 1.0)) * (ig * xc)
        for d in (1, 2, 4):
            b_cum = a_cum * _shift_rows(b_cum, d, 0.0) + b_cum
            a_cum = a_cum * _shift_rows(a_cum, d, 1.0)
        hseq = a_cum * h_prev + b_cum
        y = jax.nn.gelu(gate_s[rows, :]) * hseq
        return y, jnp.broadcast_to(hseq[SUBLANES - 1:SUBLANES, :], hseq.shape)

    def lru_scan_step(i):
        if lru_carry[0] is None:
            lru_carry[0] = lruh_s[...]
        y0, h_mid = lru_scan_block(2 * i, lru_carry[0])
        y1, lru_carry[0] = lru_scan_block(2 * i + 1, h_mid)
        yb_s[_rows(i, 2 * SUBLANES), :] = jnp.concatenate([y0, y1], axis=0).astype(BF16)
        if i == n_blk16 - 1:
            lruh_s[...] = lru_carry[0]

    def ssd_conv_step(q):
        cols = slice(q * CONV_COLS, (q + 1) * CONV_COLS)
        xbcc_s[:, cols] = jax.nn.silu(_causal_conv_cols(xbc_s, cols, scw_ref, scb_ref))

    def ssd_history_step():
        xbc_s[0:SUBLANES, :] = xbc_s[TQ:TQ + SUBLANES, :]

    pres = [dict() for _ in range(NCH)]

    def ssd_prelude_step(c):
        pre = pres[c]
        dt = jax.nn.softplus(dt_s[_rows(c, CHUNK), :] + dtb_ref[...])
        cs = dt * (-jnp.exp(alog_ref[...]))
        for d in (1, 2, 4, 8, 16, 32, 64):
            cs = cs + _shift_rows(cs, d, 0.0)
        cs_t = cs.T
        dt_t = dt.T
        pre.update(
            cs=cs, cs_t=cs_t, dt_t=dt_t,
            e_cs=jnp.exp(cs),
            w_t=dt_t * jnp.exp(cs_t[:, CHUNK - 1:CHUNK] - cs_t))

    def ssd_group_step(c, g):
        pre = pres[c]
        rws = _rows(c, CHUNK)
        cs, cs_t, dt_t, e_cs, w_t = pre["cs"], pre["cs_t"], pre["dt_t"], pre["e_cs"], pre["w_t"]
        tril = (lax.broadcasted_iota(jnp.int32, (CHUNK, CHUNK), 0)
                >= lax.broadcasted_iota(jnp.int32, (CHUNK, CHUNK), 1))
        lane_gw = lax.broadcasted_iota(jnp.int32, (CHUNK, SSD_GW), 1)
        lane_row = lax.broadcasted_iota(jnp.int32, (1, SSD_GW), 1)
        xcols = slice(g * SSD_GW, (g + 1) * SSD_GW)
        bcols = slice(D_MODEL + g * SSD_STATE, D_MODEL + (g + 1) * SSD_STATE)
        ccols = slice(D_MODEL + SSD_GROUPS * SSD_STATE + g * SSD_STATE,
                      D_MODEL + SSD_GROUPS * SSD_STATE + (g + 1) * SSD_STATE)
        xg = xbcc_s[rws, xcols]
        bg = xbcc_s[rws, bcols]
        cg = xbcc_s[rws, ccols]
        state = state_s[g]
        cb = lax.dot_general(cg.astype(BF16), bg.astype(BF16), (((1,), (1,)), ((), ())),
                             preferred_element_type=F32)
        bg_t = bg.T
        m_parts, ce_parts, bw_parts, x_parts, st_parts = [], [], [], [], []
        dec_row = jnp.zeros((1, SSD_GW), F32)
        for r in range(SSD_HPG):
            h = g * SSD_HPG + r
            seg = jnp.where(tril, cs[:, h:h + 1] - cs_t[h:h + 1, :], -jnp.inf)
            m_parts.append((cb * jnp.exp(seg) * dt_t[h:h + 1, :]).astype(BF16))
            ce_parts.append((cg * e_cs[:, h:h + 1]).astype(BF16))
            bw_parts.append((bg_t * w_t[h:h + 1, :]).astype(BF16))
            in_head = (lane_gw >= r * SSD_HEAD_DIM) & (lane_gw < (r + 1) * SSD_HEAD_DIM)
            x_parts.append(jnp.where(in_head, xg, 0.0).astype(BF16))
            st_parts.append(jnp.where(in_head, state, 0.0).astype(BF16))
            in_head_row = (lane_row >= r * SSD_HEAD_DIM) & (lane_row < (r + 1) * SSD_HEAD_DIM)
            dec_row = jnp.where(in_head_row, e_cs[CHUNK - 1:CHUNK, h:h + 1], dec_row)
        x_blk = jnp.concatenate(x_parts, axis=0)
        lhs = jnp.concatenate(m_parts + ce_parts, axis=1)
        rhs = jnp.concatenate([x_blk] + st_parts, axis=0)
        y_s[rws, xcols] = _dot(lhs, rhs)
        state_s[g] = state * dec_row + _dot(jnp.concatenate(bw_parts, axis=1), x_blk)

    def ssd_out_step(i):
        outs = []
        for half in range(2):
            rows = _rows(2 * i + half, SUBLANES)
            y = y_s[rows, :] + xbcc_s[rows, 0:D_MODEL] * dskip_ref[...]
            y = y * jax.nn.silu(z_s[rows, :])
            parts = []
            for g in range(SSD_GROUPS):
                yg = y[:, g * SSD_GW:(g + 1) * SSD_GW]
                parts.append(yg * lax.rsqrt(jnp.mean(yg * yg, axis=-1, keepdims=True) + EPS))
            outs.append(jnp.concatenate(parts, axis=1) * sng_ref[...])
        yc_s[_rows(i, 2 * SUBLANES), :] = jnp.concatenate(outs, axis=0).astype(BF16)

    def gated(rows, k):
        cols = slice(k * D_MODEL, (k + 1) * D_MODEL)
        return jax.nn.sigmoid(graw_s[rows, cols] + bgate_ref[:, cols]) * p_s[rows, cols]

    def merge01_step(i):
        rows = _rows(i, SUBLANES)
        p_s[rows, 0:D_MODEL] = gated(rows, 0) + gated(rows, 1)

    def merge2_step(i):
        halves = [p_s[_rows(2 * i + half, SUBLANES), 0:D_MODEL] + gated(_rows(2 * i + half, SUBLANES), 2)
                  for half in range(2)]
        mg_s[_rows(i, 2 * SUBLANES), :] = jnp.concatenate(halves, axis=0).astype(BF16)

    def out_step(q):
        cols = slice(q * DOT_COLS, (q + 1) * DOT_COLS)
        o_ref[0, :, cols] = x_ref[0, :, cols] + _dot(mg_s[...], wo_ref[:, cols])

    plan = _Plan()

    def add_each(fn, n, deps, mxu=0, valu=0):
        return [plan.add(functools.partial(fn, i), mxu, valu, deps) for i in range(n)]

    def add_dots(dst_ref, row0, lhs_ref, w_ref, n_cols, deps, w_col0=0, dst_col0=0):
        return [plan.add(fn, mxu=NCH * n / 2, deps=deps)
                for n, fn in dots(dst_ref, row0, lhs_ref, w_ref, n_cols, w_col0, dst_col0)]

    def piece_of(ids, col):
        return [ids[col // DOT_COLS]]

    norm = add_each(norm_step, n_blk16, (), valu=30)
    d_v = add_dots(za_s, 0, hn_s, win_ref, D_MODEL, norm, w_col0=COL_V, dst_col0=D_MODEL)
    d_u = add_dots(za_s, 0, hn_s, win_ref, D_MODEL, norm, w_col0=COL_U)
    d_xb = add_dots(xb_s, SUBLANES, hn_s, win_ref, D_MODEL, norm, w_col0=COL_XB)
    d_xbc = add_dots(xbc_s, SUBLANES, hn_s, win_ref, 2 * D_MODEL, norm, w_col0=COL_XBC)
    d_dt = add_dots(dt_s, 0, hn_s, wdt_ref, LANES, norm)
    d_gate = add_dots(gate_s, 0, hn_s, win_ref, D_MODEL, norm, w_col0=COL_GATE)
    d_z = add_dots(z_s, 0, hn_s, win_ref, D_MODEL, norm, w_col0=COL_Z)
    d_g = add_dots(graw_s, 0, hn_s, wg_ref, 3 * D_MODEL, norm)

    gv = add_each(gmlp_v_step, n_blk8, d_v, valu=70)
    gu = add_each(gmlp_u_step, n_blk8, d_u, valu=30)
    mix = [plan.add(functools.partial(gmlp_mix_step, c, g), 32, 40, gv + gu)
           for c in range(NCH) for g in range(GMLP_GROUPS)]
    p_a = add_dots(p_s, 0, ya_s, wba_ref, D_MODEL, mix)

    lconv = [plan.add(functools.partial(lru_conv_step, q), valu=110 * NCH, deps=piece_of(d_xb, q * CONV_COLS))
             for q in range(D_MODEL // CONV_COLS)]
    plan.add(lru_history_step, valu=2, deps=lconv)
    ri = [plan.add(functools.partial(ri_step, h), mxu=32 * NCH, valu=4 * NCH, deps=[lconv[h * LANES // CONV_COLS]])
          for h in range(LRU_HEADS)]
    scan = []
    for i in range(n_blk16):
        scan.append(plan.add(functools.partial(lru_scan_step, i), valu=220, deps=ri + d_gate + scan[-1:]))
    p_b = add_dots(p_s, 0, yb_s, wbb_ref, D_MODEL, scan, dst_col0=D_MODEL)

    sconv = [plan.add(functools.partial(ssd_conv_step, q), valu=160 * NCH, deps=piece_of(d_xbc, q * CONV_COLS))
             for q in range(2 * D_MODEL // CONV_COLS)]
    plan.add(ssd_history_step, valu=4, deps=sconv)
    grp = []
    for c in range(NCH):
        prelude = plan.add(functools.partial(ssd_prelude_step, c), valu=100, deps=d_dt)
        grp += [plan.add(functools.partial(ssd_group_step, c, g), 220, 250,
                         sconv + [prelude] + grp[(c - 1) * SSD_GROUPS + g:(c - 1) * SSD_GROUPS + g + 1] * (c > 0))
                for g in range(SSD_GROUPS)]
    sout = add_each(ssd_out_step, n_blk16, grp + d_z, valu=80)
    p_c = add_dots(p_s, 0, yc_s, wbc_ref, D_MODEL, sout, dst_col0=2 * D_MODEL)

    m01 = add_each(merge01_step, n_blk8, p_a + p_b + d_g, valu=27)
    m2 = add_each(merge2_step, n_blk16, m01 + p_c, valu=25)
    add_each(out_step, D_MODEL // DOT_COLS, m2, mxu=NCH * DOT_COLS / 2, valu=16 * NCH)
    plan.run()


def _mlp_kernel(x_ref, ng_ref, wup_ref, wdn_ref, fg_ref, o_ref, hn_s, *, final_norm):
    n_blk16 = MLP_TILE // (2 * SUBLANES)

    def norm_body(i, c):
        parts = []
        for half in range(2):
            x = x_ref[_rows(2 * i + half, SUBLANES), :]
            ms = jnp.mean(x * x, axis=-1, keepdims=True)
            parts.append(x * lax.rsqrt(ms + EPS) * ng_ref[...])
        hn_s[_rows(i, 2 * SUBLANES), :] = jnp.concatenate(parts, axis=0).astype(BF16)
        return c
    _unrolled(n_blk16, norm_body, 0)
    hn = hn_s[...]
    acc = x_ref[...]
    for c in range(MLP_HIDDEN // MLP_HCHUNK):
        cols = slice(c * MLP_HCHUNK, (c + 1) * MLP_HCHUNK)
        up = jnp.maximum(_dot(hn, wup_ref[:, cols]), 0.0)
        acc = acc + _dot((up * up).astype(BF16), wdn_ref[cols, :])
    if final_norm:
        ms = jnp.mean(acc * acc, axis=-1, keepdims=True)
        acc = acc * lax.rsqrt(ms + EPS) * fg_ref[...]
    o_ref[...] = acc


def _tcast_kernel(w_ref, o_ref):
    o_ref[...] = w_ref[...].T.astype(BF16)


def _transpose_cast(w_t, n_cols):
    depth, _, k = w_t.shape
    return pl.pallas_call(
        _tcast_kernel,
        out_shape=jax.ShapeDtypeStruct((depth, k, n_cols), BF16),
        grid=(depth, n_cols // TCAST_ROWS),
        in_specs=[pl.BlockSpec((None, TCAST_ROWS, k), lambda l, j: (l, j, 0))],
        out_specs=pl.BlockSpec((None, k, TCAST_ROWS), lambda l, j: (l, 0, j)),
        compiler_params=pltpu.CompilerParams(dimension_semantics=("arbitrary", "arbitrary")),
        name="tcast",
    )(w_t)


def _const_spec(shape):
    return pl.BlockSpec(shape, lambda *_: (0,) * len(shape), pipeline_mode=pl.Buffered(1))


def _layer_spec(shape, layer):
    rest = tuple(shape[1:])
    return pl.BlockSpec((None,) + rest, lambda *_: (layer,) + (0,) * len(rest),
                        pipeline_mode=pl.Buffered(1))


def _param_spec(p, layer):
    return _const_spec(p.shape) if layer is None else _layer_spec(p.shape, layer)


def _mixer_call(h, params):
    bsz, seq, _ = h.shape
    x_spec = pl.BlockSpec((1, TQ, D_MODEL), lambda b, t: (b, t, 0))
    in_specs = [x_spec] + [_param_spec(p, layer) for p, layer in params]
    scratch = [
        pltpu.VMEM((TQ, D_MODEL), BF16),
        pltpu.VMEM((TQ, 2 * D_MODEL), F32),
        pltpu.VMEM((TQ, D_MODEL), F32),
        pltpu.VMEM((TQ, D_MODEL), BF16),
        pltpu.VMEM((TQ + SUBLANES, D_MODEL), F32),
        pltpu.VMEM((TQ, D_MODEL), F32),
        pltpu.VMEM((TQ, D_MODEL), F32),
        pltpu.VMEM((TQ, 2 * D_MODEL), F32),
        pltpu.VMEM((TQ, D_MODEL), BF16),
        pltpu.VMEM((TQ, D_MODEL), F32),
        pltpu.VMEM((TQ + SUBLANES, 2 * D_MODEL), F32),
        pltpu.VMEM((TQ, 2 * D_MODEL), F32),
        pltpu.VMEM((TQ, LANES), F32),
        pltpu.VMEM((TQ, D_MODEL), F32),
        pltpu.VMEM((TQ, D_MODEL), BF16),
        pltpu.VMEM((TQ, 3 * D_MODEL), F32),
        pltpu.VMEM((TQ, 3 * D_MODEL), F32),
        pltpu.VMEM((TQ, D_MODEL), BF16),
        pltpu.VMEM((SUBLANES, D_MODEL), F32),
        pltpu.VMEM((SSD_GROUPS, SSD_STATE, SSD_GW), F32),
    ]
    return pl.pallas_call(
        _mixer_kernel,
        out_shape=jax.ShapeDtypeStruct(h.shape, F32),
        grid=(bsz, seq // TQ),
        in_specs=in_specs,
        out_specs=x_spec,
        scratch_shapes=scratch,
        compiler_params=pltpu.CompilerParams(
            dimension_semantics=("arbitrary", "arbitrary"),
            vmem_limit_bytes=VMEM_LIMIT_BYTES),
        name="mixer",
    )(h, *[p for p, _ in params])


def _mlp_call(h2d, ng, wup, wdn, layer, fg, final_norm):
    n_tok = h2d.shape[0]
    x_spec = pl.BlockSpec((MLP_TILE, D_MODEL), lambda i: (i, 0))
    return pl.pallas_call(
        functools.partial(_mlp_kernel, final_norm=final_norm),
        out_shape=jax.ShapeDtypeStruct(h2d.shape, F32),
        grid=(n_tok // MLP_TILE,),
        in_specs=[x_spec, _const_spec(ng.shape), _layer_spec(wup.shape, layer),
                  _layer_spec(wdn.shape, layer), _const_spec(fg.shape)],
        out_specs=x_spec,
        scratch_shapes=[pltpu.VMEM((MLP_TILE, D_MODEL), BF16)],
        compiler_params=pltpu.CompilerParams(
            dimension_semantics=("arbitrary",),
            vmem_limit_bytes=VMEM_LIMIT_BYTES),
        name="mlp",
    )(h2d, ng, wup, wdn, fg)


def _row(v):
    return v.reshape(1, -1).astype(F32)


def _row8(v):
    return jnp.broadcast_to(v.reshape(1, -1).astype(F32), (SUBLANES, v.size))


def _pad_lanes(v):
    return jnp.pad(v.astype(F32), ((0, 0), (0, LANES - v.shape[-1])))


def kernel(x, norm_mix_g, w_in, b_gate, gmlp_ln_g, gmlp_ln_b, gmlp_w_s, gmlp_b_s, lru_conv_w, lru_conv_b, lru_w_r, lru_b_r, lru_w_i, lru_b_i, lru_lambda, ssd_conv_w, ssd_conv_b, ssd_dt_bias, ssd_a_log, ssd_d, ssd_norm_g, w_branch_a, w_branch_b, w_branch_c, w_out, norm_mlp_g, w_mlp_up, w_mlp_down, final_norm_g):
    bsz, seq, d = x.shape
    assert d == D_MODEL and seq % TQ == 0 and (bsz * seq) % MLP_TILE == 0
    depth = w_in.shape[0]
    o_dt = COL_DT
    o_g = o_dt + SSD_HEADS
    assert w_in.shape[-1] == o_g + 3 * D_MODEL
    w_in_t = jnp.swapaxes(w_in, 1, 2)
    w_main_b = _transpose_cast(w_in_t, o_dt)
    w_g_b = _transpose_cast(w_in_t[:, o_g:, :], 3 * D_MODEL)
    w_dt_b = jnp.pad(w_in[:, :, o_dt:o_g], ((0, 0), (0, 0), (0, LANES - SSD_HEADS))).astype(BF16)
    w_ri_b = jnp.concatenate([lru_w_r, lru_w_i], axis=-1).astype(BF16)
    w_ba_b, w_bb_b, w_bc_b, w_out_b = (w.astype(BF16) for w in (w_branch_a, w_branch_b, w_branch_c, w_out))
    w_up_b, w_dn_b = w_mlp_up.astype(BF16), w_mlp_down.astype(BF16)
    h = x
    for l in range(depth):
        params = (
            (_row8(norm_mix_g[l]), None),
            (w_main_b, l),
            (w_dt_b, l),
            (w_g_b, l),
            (_row8(b_gate[l]), None),
            (_row8(gmlp_ln_g[l]), None),
            (_row8(gmlp_ln_b[l]), None),
            (gmlp_w_s, l),
            (jnp.repeat(gmlp_b_s[l].T, LANES, axis=1), None),
            (lru_conv_w, l),
            (_row(lru_conv_b[l]), None),
            (w_ri_b, l),
            (_row8(lru_b_r[l]), None),
            (_row8(lru_b_i[l]), None),
            (_row8(lru_lambda[l]), None),
            (ssd_conv_w, l),
            (_row(ssd_conv_b[l]), None),
            (_pad_lanes(_row(ssd_dt_bias[l])), None),
            (_pad_lanes(_row(ssd_a_log[l])), None),
            (_row8(jnp.repeat(ssd_d[l], SSD_HEAD_DIM)), None),
            (_row8(ssd_norm_g[l]), None),
            (w_ba_b, l),
            (w_bb_b, l),
            (w_bc_b, l),
            (w_out_b, l),
        )
        h = _mixer_call(h, params)
        h2d = _mlp_call(h.reshape(bsz * seq, d), _row8(norm_mlp_g[l]), w_up_b, w_dn_b, l,
                        _row(final_norm_g), final_norm=(l == depth - 1))
        h = h2d.reshape(bsz, seq, d)
    return h
```
